```python
import math
import jax, jax.numpy as jnp
from jax import lax
import numpy as np

D_MODEL = 1024
BATCH = 8
SEQ = 2048
DEPTH = 4
DEC_BATCH = 128
DEC_SEQ = 8
PAST_LEN = 16384
PAGE_SIZE = 128

N_BRANCH = 4
BRANCH_WIDTH = 512
H_A = 4
DK_A = 128
DV_A = 128
HK_A = H_A * DK_A
HV_A = H_A * DV_A
H_B = 4
DK_B = 128
DV_B = 128
HK_B = H_B * DK_B
HV_B = H_B * DV_B
D_C = BRANCH_WIDTH
CONV_C = 3
D_D = BRANCH_WIDTH
CONV_D = 31
D_FF = 4 * D_MODEL
CHUNK = 64
EPS = 1e-6
SPLIT_SIZES = (HK_A, HK_A, HV_A, HV_A, HK_B, HK_B, HV_B, HV_B, H_B, H_B, D_C, D_C, D_C, D_D, D_D, N_BRANCH * D_MODEL)
D_IN = 2 * HK_A + 2 * HV_A + 2 * HK_B + 2 * HV_B + 2 * H_B + 3 * D_C + 2 * D_D + N_BRANCH * D_MODEL

kernel_name = 'hybrid_hgrn2_mlstm_conv_decoder_step'


def _split_points():
    pts, acc = [], 0
    for s in SPLIT_SIZES[:-1]:
        acc += s
        pts.append(acc)
    return pts


def rms_norm(x, g):
    xf = x.astype(jnp.float32)
    y = xf * lax.rsqrt(jnp.mean(xf * xf, axis=-1, keepdims=True) + EPS)
    return (y * g.astype(jnp.float32)).astype(x.dtype)


def head_rms_norm(o, g, n_heads):
    y = o * lax.rsqrt(jnp.mean(o * o, axis=-1, keepdims=True) + EPS)
    return y * g.astype(jnp.float32).reshape(n_heads, -1)


def layer_norm(x, g, b):
    xf = x.astype(jnp.float32)
    mu = jnp.mean(xf, axis=-1, keepdims=True)
    xc = xf - mu
    y = xc * lax.rsqrt(jnp.mean(xc * xc, axis=-1, keepdims=True) + EPS)
    return y * g.astype(jnp.float32) + b.astype(jnp.float32)


def causal_depthwise_conv(u, buf, w):
    width, ch = w.shape
    ext = jnp.concatenate([buf.astype(u.dtype), u], axis=1)
    y = lax.conv_general_dilated(ext, w[:, None, :].astype(u.dtype), (1,), 'VALID',
                                 dimension_numbers=('NWC', 'WIO', 'NWC'), feature_group_count=ch)
    return y, ext[:, ext.shape[1] - (width - 1):]


def _chunks(a, n_chunks):
    b, t = a.shape[:2]
    return jnp.moveaxis(a.reshape(b, n_chunks, t // n_chunks, *a.shape[2:]), 1, 0)


def _unchunk(a):
    n, b, l = a.shape[:3]
    return jnp.moveaxis(a, 0, 1).reshape(b, n * l, *a.shape[3:])


def hgrn2_recurrence(q, k, v, log_f, s0):
    t = q.shape[1]
    length = math.gcd(t, CHUNK)
    n = t // length
    causal = jnp.tril(jnp.ones((length, length), dtype=bool))

    def chunk_step(s, xs):
        qc, kc, vc, gc = xs
        g_cum = jnp.cumsum(gc, axis=1)
        rel = g_cum[:, :, None] - g_cum[:, None]
        decay = jnp.exp(jnp.where(causal[None, :, :, None, None], rel, -jnp.inf))
        scores = jnp.einsum('blhd,bmhd,blmhd->bhlm', qc, kc, decay)
        o = (jnp.einsum('bhlm,bmhv->blhv', scores, vc)
             + jnp.einsum('blhd,bhdv->blhv', qc * jnp.exp(g_cum), s))
        g_last = g_cum[:, -1]
        k_dec = kc * jnp.exp(g_last[:, None] - g_cum)
        s_new = jnp.exp(g_last)[..., None] * s + jnp.einsum('blhd,blhv->bhdv', k_dec, vc)
        return s_new, o

    xs = (_chunks(q, n), _chunks(k, n), _chunks(v, n), _chunks(log_f, n))
    s_fin, o = lax.scan(chunk_step, s0, xs)
    return _unchunk(o), s_fin


def mlstm_recurrence(q, k, v, i_pre, log_f, c0, n0, m0):
    t = q.shape[1]
    length = math.gcd(t, CHUNK)
    n = t // length
    causal = jnp.tril(jnp.ones((length, length), dtype=bool))

    def chunk_step(carry, xs):
        c, nv, m = carry
        qc, kc, vc, ic, fc = xs
        b = jnp.cumsum(fc, axis=1)
        d = jnp.where(causal[None, :, :, None], b[:, :, None] - b[:, None] + ic[:, None], -jnp.inf)
        m_inter = b + m[:, None]
        m_t = jnp.maximum(m_inter, jnp.max(d, axis=2))
        w = jnp.exp(d - m_t[:, :, None])
        w_inter = jnp.exp(m_inter - m_t)
        qk = jnp.einsum('blhd,bmhd->blmh', qc, kc) * w
        num = (jnp.einsum('blmh,bmhv->blhv', qk, vc)
               + w_inter[..., None] * jnp.einsum('blhd,bhdv->blhv', qc, c))
        den = qk.sum(axis=2) + w_inter * jnp.einsum('blhd,bhd->blh', qc, nv)
        h = num / jnp.maximum(jnp.abs(den), jnp.exp(-m_t))[..., None]
        m_new = m_t[:, -1]
        w_state = jnp.exp(b[:, -1:] - b + ic - m_new[:, None])
        carry_decay = jnp.exp(b[:, -1] + m - m_new)
        c_new = carry_decay[..., None, None] * c + jnp.einsum('blh,blhd,blhv->bhdv', w_state, kc, vc)
        n_new = carry_decay[..., None] * nv + jnp.einsum('blh,blhd->bhd', w_state, kc)
        return (c_new, n_new, m_new), h

    xs = (_chunks(q, n), _chunks(k, n), _chunks(v, n), _chunks(i_pre, n), _chunks(log_f, n))
    (c_fin, n_fin, m_fin), h = lax.scan(chunk_step, (c0, n0, m0), xs)
    return _unchunk(h), c_fin, n_fin, m_fin


def hybrid_mixer(h, lw, st):
    (w_in, w_branch, w_out, lb, norm_hgrn, i_bias, f_bias, norm_mlstm, conv_short_w,
     conv_conf_w, conv_conf_b, ln_conf_g, ln_conf_b) = lw
    s_hgrn, s_c, s_n, s_m, buf_short, buf_conf = st
    bt, t, _ = h.shape
    f32 = jnp.float32
    proj = h @ w_in
    (qa, fa, ia, ga, qb, kb, vb, ob, igb, fgb, bc, cc, xc, ad, gd, gates) = jnp.split(proj, _split_points(), axis=-1)

    forget = lb + (1.0 - lb) * jax.nn.sigmoid(qa.astype(f32) * 0.0 + fa.astype(f32)) if False else lb + (1.0 - lb) * jax.nn.sigmoid(fa.astype(f32))
    log_fa = jnp.log(forget).reshape(bt, t, H_A, DK_A)
    k_a = (1.0 - forget).reshape(bt, t, H_A, DK_A)
    q_a = jax.nn.silu(qa.astype(f32)).reshape(bt, t, H_A, DK_A)
    v_a = ia.astype(f32).reshape(bt, t, H_A, DV_A)
    o_a, s_hgrn_new = hgrn2_recurrence(q_a, k_a, v_a, log_fa, s_hgrn.astype(f32))
    o_a = (head_rms_norm(o_a, norm_hgrn, H_A).reshape(bt, t, HV_A) * jax.nn.silu(ga.astype(f32))).astype(h.dtype)

    q_b = qb.astype(f32).reshape(bt, t, H_B, DK_B) * (DK_B ** -0.5)
    k_b = kb.astype(f32).reshape(bt, t, H_B, DK_B)
    v_b = vb.astype(f32).reshape(bt, t, H_B, DV_B)
    i_pre = igb.astype(f32) + i_bias.astype(f32)
    log_fb = jax.nn.log_sigmoid(fgb.astype(f32) + f_bias.astype(f32))
    h_b, c_new, n_new, m_new = mlstm_recurrence(q_b, k_b, v_b, i_pre, log_fb,
                                                s_c.astype(f32), s_n.astype(f32), s_m.astype(f32))
    o_b = (jax.nn.sigmoid(ob.astype(f32)) * head_rms_norm(h_b, norm_mlstm, H_B).reshape(bt, t, HV_B)).astype(h.dtype)

    y_c, buf_short_new = causal_depthwise_conv(cc * xc, buf_short, conv_short_w)
    o_c = (bc * y_c).astype(h.dtype)

    glu = ad * jax.nn.sigmoid(gd)
    y_d, buf_conf_new = causal_depthwise_conv(glu, buf_conf, conv_conf_w)
    o_d = jax.nn.silu(layer_norm(y_d + conv_conf_b.astype(y_d.dtype), ln_conf_g, ln_conf_b)).astype(h.dtype)

    branches = jnp.stack([o_a, o_b, o_c, o_d], axis=2)
    proj_br = jnp.einsum('btnc,ncd->btnd', branches, w_branch)
    gate = jax.nn.sigmoid(gates.reshape(bt, t, N_BRANCH, D_MODEL))
    merged = jnp.sum(gate * proj_br, axis=2)
    out = merged @ w_out
    return out, (s_hgrn_new, c_new, n_new, m_new, buf_short_new, buf_conf_new)


def trunk_layer(x, lw, norm_mix, norm_mlp, w_up, w_down, st):
    y, st_new = hybrid_mixer(rms_norm(x, norm_mix), lw, st)
    x = x + y
    hm = rms_norm(x, norm_mlp)
    x = x + jnp.square(jax.nn.relu(hm @ w_up)) @ w_down
    return x, st_new


def setup_inputs(seed: int = 0) -> dict:
    key = jax.random.key(seed)
    ks = jax.random.split(key, 26)

    def nrm(k, shape, scale):
        return jax.random.normal(k, shape, jnp.float32) * scale

    return {
        'x_prompt': nrm(ks[0], (BATCH, SEQ, D_MODEL), 1.0),
        'x_sample': nrm(ks[1], (DEC_BATCH, DEC_SEQ, D_MODEL), 1.0),
        'state_hgrn': nrm(ks[2], (DEPTH, DEC_BATCH, H_A, DK_A, DV_A), 0.3),
        'state_mlstm_c': nrm(ks[3], (DEPTH, DEC_BATCH, H_B, DK_B, DV_B), 0.1),
        'state_mlstm_n': nrm(ks[4], (DEPTH, DEC_BATCH, H_B, DK_B), 0.3),
        'state_mlstm_m': jax.random.uniform(ks[5], (DEPTH, DEC_BATCH, H_B), jnp.float32, 0.0, 4.0),
        'state_conv_short': nrm(ks[6], (DEPTH, DEC_BATCH, CONV_C - 1, D_C), 0.5),
        'state_conv_conformer': nrm(ks[7], (DEPTH, DEC_BATCH, CONV_D - 1, D_D), 0.5),
        'w_in': nrm(ks[8], (DEPTH, D_MODEL, D_IN), D_MODEL ** -0.5),
        'w_branch': nrm(ks[9], (DEPTH, N_BRANCH, BRANCH_WIDTH, D_MODEL), BRANCH_WIDTH ** -0.5),
        'w_out': nrm(ks[10], (DEPTH, D_MODEL, D_MODEL), D_MODEL ** -0.5),
        'lb_logits': nrm(ks[11], (DEPTH, HK_A), 0.5),
        'norm_hgrn': 1.0 + nrm(ks[12], (DEPTH, HV_A), 0.1),
        'mlstm_i_bias': nrm(ks[13], (DEPTH, H_B), 0.1),
        'mlstm_f_bias': jnp.linspace(3.0, 6.0, H_B, dtype=jnp.float32)[None, :] + nrm(ks[14], (DEPTH, H_B), 0.1),
        'norm_mlstm': 1.0 + nrm(ks[15], (DEPTH, HV_B), 0.1),
        'conv_short_w': nrm(ks[16], (DEPTH, CONV_C, D_C), CONV_C ** -0.5),
        'conv_conformer_w': nrm(ks[17], (DEPTH, CONV_D, D_D), CONV_D ** -0.5),
        'conv_conformer_b': nrm(ks[18], (DEPTH, D_D), 0.02),
        'ln_conformer_g': 1.0 + nrm(ks[19], (DEPTH, D_D), 0.1),
        'ln_conformer_b': nrm(ks[20], (DEPTH, D_D), 0.02),
        'norm_mix': 1.0 + nrm(ks[21], (DEPTH, D_MODEL), 0.1),
        'norm_mlp': 1.0 + nrm(ks[22], (DEPTH, D_MODEL), 0.1),
        'w_up': nrm(ks[23], (DEPTH, D_MODEL, D_FF), D_MODEL ** -0.5),
        'w_down': nrm(ks[24], (DEPTH, D_FF, D_MODEL), D_FF ** -0.5),
        'norm_final': 1.0 + nrm(ks[25], (D_MODEL,), 0.1),
    }


def reference(x_prompt, x_sample, state_hgrn, state_mlstm_c, state_mlstm_n, state_mlstm_m,
              state_conv_short, state_conv_conformer, w_in, w_branch, w_out, lb_logits, norm_hgrn,
              mlstm_i_bias, mlstm_f_bias, norm_mlstm, conv_short_w, conv_conformer_w, conv_conformer_b,
              ln_conformer_g, ln_conformer_b, norm_mix, norm_mlp, w_up, w_down, norm_final):
    lb_soft = jax.nn.softmax(lb_logits.astype(jnp.float32), axis=0)
    lower_bounds = jnp.cumsum(lb_soft, axis=0) - lb_soft[0]

    bp = x_prompt.shape[0]
    xp, xs = x_prompt, x_sample
    new_p, new_s = [], []
    for l in range(DEPTH):
        lw = (w_in[l], w_branch[l], w_out[l], lower_bounds[l], norm_hgrn[l], mlstm_i_bias[l], mlstm_f_bias[l],
              norm_mlstm[l], conv_short_w[l], conv_conformer_w[l], conv_conformer_b[l],
              ln_conformer_g[l], ln_conformer_b[l])
        zero_st = (jnp.zeros((bp, H_A, DK_A, DV_A), jnp.float32),
                   jnp.zeros((bp, H_B, DK_B, DV_B), jnp.float32),
                   jnp.zeros((bp, H_B, DK_B), jnp.float32),
                   jnp.zeros((bp, H_B), jnp.float32),
                   jnp.zeros((bp, CONV_C - 1, D_C), x_prompt.dtype),
                   jnp.zeros((bp, CONV_D - 1, D_D), x_prompt.dtype))
        past_st = (state_hgrn[l], state_mlstm_c[l], state_mlstm_n[l], state_mlstm_m[l],
                   state_conv_short[l], state_conv_conformer[l])
        xp, st_p = trunk_layer(xp, lw, norm_mix[l], norm_mlp[l], w_up[l], w_down[l], zero_st)
        xs, st_s = trunk_layer(xs, lw, norm_mix[l], norm_mlp[l], w_up[l], w_down[l], past_st)
        new_p.append(st_p)
        new_s.append(st_s)

    y_prompt = rms_norm(xp, norm_final)
    y_sample = rms_norm(xs, norm_final)
    hgrn_p = jnp.stack([s[0] for s in new_p])
    hgrn_s = jnp.stack([s[0] for s in new_s])
    mlstm_c_p = jnp.stack([s[1] for s in new_p])
    mlstm_c_s = jnp.stack([s[1] for s in new_s])
    mlstm_n_p = jnp.stack([s[2] for s in new_p])
    mlstm_n_s = jnp.stack([s[2] for s in new_s])
    mlstm_m_p = jnp.stack([s[3] for s in new_p])
    mlstm_m_s = jnp.stack([s[3] for s in new_s])
    conv_short_p = jnp.stack([s[4] for s in new_p])
    conv_short_s = jnp.stack([s[4] for s in new_s])
    conv_conf_p = jnp.stack([s[5] for s in new_p])
    conv_conf_s = jnp.stack([s[5] for s in new_s])
    return (y_prompt, y_sample, hgrn_p, hgrn_s, mlstm_c_p, mlstm_c_s, mlstm_n_p, mlstm_n_s,
            mlstm_m_p, mlstm_m_s, conv_short_p, conv_short_s, conv_conf_p, conv_conf_s)
```

```python
import functools
import math

import jax
import jax.numpy as jnp
from jax import lax
from jax.experimental import pallas as pl
from jax.experimental.pallas import tpu as pltpu

F32 = jnp.float32
BF16 = jnp.bfloat16

D_MODEL = 1024
N_HEAD = 4
D_HEAD = 128
D_BR = 512
N_BRANCH = 4
CONV_C = 3
CONV_D = 31
D_FF = 4 * D_MODEL
CHUNK = 64
EPS = 1e-6
SUBLANES = 8
LANES = 128

N_MAIN = 8 * D_BR
N_GATE = N_BRANCH * D_MODEL
N_CONV = 5 * D_BR
N_SMALL = LANES
D_PROJ = N_MAIN + N_GATE + N_CONV + N_SMALL
COL_GATE = N_MAIN
COL_CONV = N_MAIN + N_GATE
COL_SMALL = N_MAIN + N_GATE + N_CONV
PROJ_TN = D_PROJ // 5

VMEM_LIMIT = 56 * 1024 * 1024


def _cparams(sem):
    return pltpu.CompilerParams(dimension_semantics=sem, vmem_limit_bytes=VMEM_LIMIT)


def _dot(a, b):
    return jnp.dot(a.astype(BF16), b.astype(BF16), preferred_element_type=F32)


def _dot_nt(a, b):
    return lax.dot_general(a.astype(BF16), b.astype(BF16), (((1,), (1,)), ((), ())),
                           preferred_element_type=F32)


def _dot_tn(a, b):
    return lax.dot_general(a.astype(BF16), b.astype(BF16), (((0,), (0,)), ((), ())),
                           preferred_element_type=F32)


def _split3(x):
    hi = x.astype(BF16)
    r = x - hi.astype(F32)
    mid = r.astype(BF16)
    lo = (r - mid.astype(F32)).astype(BF16)
    return hi, mid, lo


def _sigmoid(x):
    return jax.nn.sigmoid(x)


def _log_sigmoid(x):
    return jnp.minimum(x, 0.0) - jnp.log1p(jnp.exp(-jnp.abs(x)))


def _rms(x, g):
    return x * lax.rsqrt(jnp.mean(x * x, axis=-1, keepdims=True) + EPS) * g


def _lb_kernel(x_ref, o_ref):
    x = x_ref[...]
    depth = x.shape[0]
    e = jnp.exp(x - jnp.max(x, axis=0, keepdims=True))
    s = e / jnp.sum(e, axis=0, keepdims=True)
    acc = jnp.zeros_like(s[0:1])
    for i in range(depth):
        acc = acc + s[i:i + 1]
        o_ref[i:i + 1, :] = acc - s[0:1]


def _lower_bounds(lb_logits):
    return pl.pallas_call(
        _lb_kernel, out_shape=jax.ShapeDtypeStruct(lb_logits.shape, F32), name="lower_bounds",
    )(lb_logits.astype(F32))


def _proj_kernel(x_ref, g_ref, w_ref, o_ref):
    h = _rms(x_ref[...], g_ref[...])
    o_ref[...] = jnp.dot(h.astype(BF16), w_ref[...], preferred_element_type=F32)


def _proj(x, g, w):
    n = x.shape[0]
    tm = min(n, 512)
    return pl.pallas_call(
        _proj_kernel,
        grid=(D_PROJ // PROJ_TN, n // tm),
        in_specs=[pl.BlockSpec((tm, D_MODEL), lambda j, i: (i, 0)),
                  pl.BlockSpec((1, D_MODEL), lambda j, i: (0, 0)),
                  pl.BlockSpec((D_MODEL, PROJ_TN), lambda j, i: (0, j))],
        out_specs=pl.BlockSpec((tm, PROJ_TN), lambda j, i: (i, j)),
        out_shape=jax.ShapeDtypeStruct((n, D_PROJ), F32),
        compiler_params=_cparams(("parallel", "parallel")),
        name="proj",
    )(x, g, w)


def _cumsum_rows(x):
    n = x.shape[0]
    row = lax.broadcasted_iota(jnp.int32, x.shape, 0)
    s = 1
    while s < n:
        x = x + jnp.where(row >= s, pltpu.roll(x, s, 0), 0.0)
        s *= 2
    return x


def _group_row(x, group, j):
    n, w = x.shape
    x3 = x.reshape(n // group, group, w)
    return jnp.broadcast_to(x3[:, j:j + 1, :], x3.shape).reshape(n, w)


def _hgrn_kernel(qa_ref, fa_ref, ia_ref, ga_ref, lb_ref, nw_ref, s0_ref, o_ref, s_ref, *, L):
    bblk, tt, _ = qa_ref.shape
    nchunk = tt // L
    levels = [c for c in (32, 16, 8) if 2 * c <= L]

    @pl.when(pl.program_id(1) == 0)
    def _():
        s_ref[...] = s0_ref[...]

    lb = lb_ref[...]
    nw = nw_ref[...]
    row = lax.broadcasted_iota(jnp.int32, (L, D_BR), 0)
    row_l = lax.broadcasted_iota(jnp.int32, (L, L), 0)
    col_l = lax.broadcasted_iota(jnp.int32, (L, L), 1)

    def body(idx, carry):
        bi = idx // nchunk
        r0 = pl.multiple_of((idx % nchunk) * L, L)
        rows = pl.ds(r0, L)
        f = lb + (1.0 - lb) * _sigmoid(fa_ref[bi, rows, :])
        g = jnp.log(f)
        k = 1.0 - f
        qa = qa_ref[bi, rows, :]
        q = qa * _sigmoid(qa)
        v = ia_ref[bi, rows, :]
        gc = _cumsum_rows(g)

        heads = [slice(h * D_HEAD, (h + 1) * D_HEAD) for h in range(N_HEAD)]
        qg = q * jnp.exp(gc)
        out = [_dot(qg[:, hs], s_ref[bi, h]) for h, hs in enumerate(heads)]

        scores = [jnp.zeros((L, L), F32) for _ in heads]
        for c in levels:
            gb = _group_row(gc, 2 * c, c - 1)
            upper = (row // c) % 2 == 1
            e = jnp.exp(jnp.where(upper, gc - gb, gb - gc))
            qe = jnp.where(upper, q * e, 0.0)
            ke = jnp.where(upper, 0.0, k * e)
            same = (row_l // (2 * c)) == (col_l // (2 * c))
            for h, hs in enumerate(heads):
                scores[h] = scores[h] + jnp.where(same, _dot_nt(qe[:, hs], ke[:, hs]), 0.0)
        for j in range(SUBLANES):
            kj = _group_row(k, SUBLANES, j)
            gj = _group_row(gc, SUBLANES, j)
            p = jnp.where(row % SUBLANES >= j, q * kj * jnp.exp(jnp.minimum(gc - gj, 0.0)), 0.0)
            hit = col_l == (row_l // SUBLANES) * SUBLANES + j
            for h, hs in enumerate(heads):
                scores[h] = scores[h] + jnp.where(hit, jnp.sum(p[:, hs], axis=-1, keepdims=True), 0.0)

        g_last = gc[L - 1:L, :]
        k_dec = k * jnp.exp(g_last - gc)
        decay = jnp.exp(g_last)
        ga = ga_ref[bi, rows, :]
        gate = ga * _sigmoid(ga)
        for h, hs in enumerate(heads):
            o = out[h] + _dot(scores[h], v[:, hs])
            decay_col = jnp.broadcast_to(decay[:, hs], (D_HEAD, D_HEAD)).T
            s_ref[bi, h] = decay_col * s_ref[bi, h] + _dot_tn(k_dec[:, hs], v[:, hs])
            y = o * lax.rsqrt(jnp.mean(o * o, axis=-1, keepdims=True) + EPS) * nw[:, hs]
            o_ref[bi, rows, hs] = (y * gate[:, hs]).astype(o_ref.dtype)
        return carry

    lax.fori_loop(0, bblk * nchunk, body, 0)


def _hgrn(proj3, lb, nw, s0, L, bblk, tt):
    b, t, _ = proj3.shape
    col = lambda c: pl.BlockSpec((bblk, tt, D_BR), lambda i, j: (i, j, c))
    vec = pl.BlockSpec((1, D_BR), lambda i, j: (0, 0))
    st = pl.BlockSpec((bblk, N_HEAD, D_HEAD, D_HEAD), lambda i, j: (i, 0, 0, 0))
    return pl.pallas_call(
        functools.partial(_hgrn_kernel, L=L),
        grid=(b // bblk, t // tt),
        in_specs=[col(0), col(1), col(2), col(3), vec, vec, st],
        out_specs=[pl.BlockSpec((bblk, tt, D_BR), lambda i, j: (i, j, 0)), st],
        out_shape=[jax.ShapeDtypeStruct((b, t, D_BR), BF16),
                   jax.ShapeDtypeStruct(s0.shape, F32)],
        compiler_params=_cparams(("parallel", "arbitrary")),
        name="hgrn",
    )(proj3, proj3, proj3, proj3, lb, nw, s0)


def _mlstm_kernel(q_ref, k_ref, v_ref, og_ref, sc_ref, sr_ref, bc_ref, br_ref, nw_ref,
                  c0_ref, n0_ref, m0_ref, o_ref, c_ref, n_ref, m_ref, *, L):
    bblk, tt, _ = q_ref.shape
    nchunk = tt // L

    @pl.when(pl.program_id(1) == 0)
    def _():
        c_ref[...] = c0_ref[...]
        n_ref[...] = n0_ref[...]
        m_ref[...] = m0_ref[...]

    nw = nw_ref[...]
    bias_c = bc_ref[...]
    bias_r = br_ref[...]
    row_l = lax.broadcasted_iota(jnp.int32, (L, L), 0)
    col_l = lax.broadcasted_iota(jnp.int32, (L, L), 1)
    causal = row_l >= col_l
    tri = jnp.where(causal, 1.0, 0.0).astype(BF16)
    tri_t = jnp.where(row_l <= col_l, 1.0, 0.0).astype(BF16)
    scale = D_HEAD ** -0.5

    def body(idx, carry):
        bi = idx // nchunk
        ci = idx % nchunk
        r0 = pl.multiple_of(ci * L, L)
        rows = pl.ds(r0, L)
        pre_c = sc_ref[bi, rows, :] + bias_c
        pre_r = sr_ref[bi, ci] + bias_r
        hi, mid, lo = _split3(_log_sigmoid(pre_c))
        b_c = (jnp.dot(tri, hi, preferred_element_type=F32) + jnp.dot(tri, mid, preferred_element_type=F32)
               + jnp.dot(tri, lo, preferred_element_type=F32))
        hi, mid, lo = _split3(_log_sigmoid(pre_r))
        b_r = (jnp.dot(hi, tri_t, preferred_element_type=F32) + jnp.dot(mid, tri_t, preferred_element_type=F32)
               + jnp.dot(lo, tri_t, preferred_element_type=F32))
        og = og_ref[bi, rows, :]
        for h in range(N_HEAD):
            hs = slice(h * D_HEAD, (h + 1) * D_HEAD)
            q = q_ref[bi, rows, hs] * scale
            k = k_ref[bi, rows, hs]
            v = v_ref[bi, rows, hs]
            c_old = c_ref[bi, h]
            n_old = n_ref[bi, h]
            m_old = m_ref[bi, h]
            b_col = b_c[:, N_HEAD + h:N_HEAD + h + 1]
            i_col = pre_c[:, h:h + 1]
            b_row = b_r[N_HEAD + h:N_HEAD + h + 1, :]
            i_row = pre_r[h:h + 1, :]
            d = jnp.where(causal, b_col + (i_row - b_row), -jnp.inf)
            m_inter = b_col + m_old
            m_t = jnp.maximum(m_inter, jnp.max(d, axis=1, keepdims=True))
            w = jnp.exp(d - m_t)
            w_inter = jnp.exp(m_inter - m_t)
            qk = _dot_nt(q, k) * w
            num = _dot(qk, v) + w_inter * _dot(q, c_old)
            den = jnp.sum(qk, axis=1, keepdims=True) + w_inter * jnp.sum(q * n_old, axis=1, keepdims=True)
            hh = num / jnp.maximum(jnp.abs(den), jnp.exp(-m_t))
            m_new = m_t[L - 1:L, :]
            b_last = b_col[L - 1:L, :]
            w_state = jnp.exp(b_last - b_col + i_col - m_new)
            carry_decay = jnp.exp(b_last + m_old - m_new)
            kw = w_state * k
            c_ref[bi, h] = carry_decay * c_old + _dot_tn(kw, v)
            n_ref[bi, h] = carry_decay * n_old + jnp.sum(kw, axis=0, keepdims=True)
            m_ref[bi, h] = m_new
            y = hh * lax.rsqrt(jnp.mean(hh * hh, axis=-1, keepdims=True) + EPS) * nw[:, hs]
            o_ref[bi, rows, hs] = (_sigmoid(og[:, hs]) * y).astype(o_ref.dtype)
        return carry

    lax.fori_loop(0, bblk * nchunk, body, 0)


def _mlstm(proj3, small_row, bias_c, bias_r, nw, c0, n0, m0, L, bblk, tt):
    b, t, _ = proj3.shape
    col = lambda c: pl.BlockSpec((bblk, tt, D_BR), lambda i, j: (i, j, c))
    st4 = lambda shape: pl.BlockSpec((bblk,) + shape, lambda i, j: (i, 0, 0, 0))
    full2 = lambda shape: pl.BlockSpec(shape, lambda i, j: (0, 0))
    return pl.pallas_call(
        functools.partial(_mlstm_kernel, L=L),
        grid=(b // bblk, t // tt),
        in_specs=[col(4), col(5), col(6), col(7),
                  pl.BlockSpec((bblk, tt, N_SMALL), lambda i, j: (i, j, COL_SMALL // N_SMALL)),
                  pl.BlockSpec((bblk, tt // L, 2 * N_HEAD, L), lambda i, j: (i, j, 0, 0)),
                  full2((1, N_SMALL)), full2((2 * N_HEAD, 1)), full2((1, D_BR)),
                  st4((N_HEAD, D_HEAD, D_HEAD)), st4((N_HEAD, 1, D_HEAD)), st4((N_HEAD, 1, 1))],
        out_specs=[pl.BlockSpec((bblk, tt, D_BR), lambda i, j: (i, j, 0)),
                   st4((N_HEAD, D_HEAD, D_HEAD)), st4((N_HEAD, 1, D_HEAD)), st4((N_HEAD, 1, 1))],
        out_shape=[jax.ShapeDtypeStruct((b, t, D_BR), BF16),
                   jax.ShapeDtypeStruct(c0.shape, F32),
                   jax.ShapeDtypeStruct(n0.shape, F32),
                   jax.ShapeDtypeStruct(m0.shape, F32)],
        compiler_params=_cparams(("parallel", "arbitrary")),
        name="mlstm",
    )(proj3, proj3, proj3, proj3, proj3, small_row, bias_c, bias_r, nw, c0, n0, m0)


PAD_C = 8
PAD_D = 32
CONV_ROWS = 32


def _conv_kernel(bc_ref, cc_ref, xc_ref, ad_ref, gd_ref, bufc_ref, bufd_ref, wc_ref, wd_ref, bd_ref,
                 lg_ref, lbias_ref, oc_ref, od_ref, nbc_ref, nbd_ref, extc, extd):
    bblk, tt, _ = bc_ref.shape
    hc, hd = CONV_C - 1, CONV_D - 1
    rt = min(tt, CONV_ROWS)

    @pl.when(pl.program_id(1) == 0)
    def _():
        extc[:, PAD_C - hc:PAD_C, :] = bufc_ref[...]
        extd[:, PAD_D - hd:PAD_D, :] = bufd_ref[...]

    extc[:, PAD_C:PAD_C + tt, :] = cc_ref[...] * xc_ref[...]
    extd[:, PAD_D:PAD_D + tt, :] = ad_ref[...] * _sigmoid(gd_ref[...])

    def body(idx, carry):
        bi = idx // (tt // rt)
        r0 = pl.multiple_of((idx % (tt // rt)) * rt, rt)
        win = extc[bi, pl.ds(r0, rt + PAD_C), :]
        acc = jnp.zeros((rt, D_BR), F32)
        for j in range(CONV_C):
            off = PAD_C - hc + j
            acc = acc + wc_ref[j:j + 1, :] * win[off:off + rt]
        oc_ref[bi, pl.ds(r0, rt), :] = (bc_ref[bi, pl.ds(r0, rt), :] * acc).astype(oc_ref.dtype)
        win = extd[bi, pl.ds(r0, rt + PAD_D), :]
        acc = jnp.zeros((rt, D_BR), F32)
        for res in range(SUBLANES):
            taps = [j for j in range(CONV_D) if (PAD_D - hd + j) % SUBLANES == res]
            span = max(PAD_D - hd + j for j in taps) - res + rt
            shifted = win[res:res + span]
            for j in taps:
                off = PAD_D - hd + j - res
                acc = acc + wd_ref[j:j + 1, :] * shifted[off:off + rt]
        y = acc + bd_ref[...]
        yc = y - jnp.mean(y, axis=-1, keepdims=True)
        z = yc * lax.rsqrt(jnp.mean(yc * yc, axis=-1, keepdims=True) + EPS) * lg_ref[...] + lbias_ref[...]
        od_ref[bi, pl.ds(r0, rt), :] = (z * _sigmoid(z)).astype(od_ref.dtype)
        return carry

    lax.fori_loop(0, bblk * (tt // rt), body, 0)

    new_c = extc[:, PAD_C + tt - hc:PAD_C + tt, :]
    new_d = extd[:, PAD_D + tt - hd:PAD_D + tt, :]
    nbc_ref[...] = new_c
    nbd_ref[...] = new_d
    extc[:, PAD_C - hc:PAD_C, :] = new_c
    extd[:, PAD_D - hd:PAD_D, :] = new_d


def _conv(proj3, bufc, bufd, wc, wd, bd, lg, lbias, bblk, tt):
    b, t, _ = proj3.shape
    c0 = COL_CONV // D_BR
    col = lambda c: pl.BlockSpec((bblk, tt, D_BR), lambda i, j: (i, j, c0 + c))
    full2 = lambda shape: pl.BlockSpec(shape, lambda i, j: (0, 0))
    hist = lambda n: pl.BlockSpec((bblk, n, D_BR), lambda i, j: (i, 0, 0))
    outc = pl.BlockSpec((bblk, tt, D_BR), lambda i, j: (i, j, 0))
    return pl.pallas_call(
        _conv_kernel,
        grid=(b // bblk, t // tt),
        in_specs=[col(0), col(1), col(2), col(3), col(4), hist(CONV_C - 1), hist(CONV_D - 1),
                  full2((CONV_C, D_BR)), full2((CONV_D, D_BR)), full2((1, D_BR)), full2((1, D_BR)),
                  full2((1, D_BR))],
        out_specs=[outc, outc, hist(CONV_C - 1), hist(CONV_D - 1)],
        out_shape=[jax.ShapeDtypeStruct((b, t, D_BR), BF16), jax.ShapeDtypeStruct((b, t, D_BR), BF16),
                   jax.ShapeDtypeStruct(bufc.shape, F32), jax.ShapeDtypeStruct(bufd.shape, F32)],
        scratch_shapes=[pltpu.VMEM((bblk, PAD_C + tt, D_BR), F32), pltpu.VMEM((bblk, PAD_D + tt, D_BR), F32)],
        compiler_params=_cparams(("parallel", "arbitrary")),
        name="conv",
    )(proj3, proj3, proj3, proj3, proj3, bufc, bufd, wc, wd, bd, lg, lbias)


FF_TILE = 1024


def _mix_kernel(x_ref, oa_ref, ob_ref, oc_ref, od_ref, gt_ref, wb_ref, wo_ref, nm_ref, wu_ref, wdn_ref,
                nf_ref, o_ref, *, final_norm):
    merged = None
    for n, br in enumerate((oa_ref, ob_ref, oc_ref, od_ref)):
        gate = _sigmoid(gt_ref[:, n * D_MODEL:(n + 1) * D_MODEL])
        term = gate * jnp.dot(br[...], wb_ref[n], preferred_element_type=F32)
        merged = term if merged is None else merged + term
    x = x_ref[...] + jnp.dot(merged.astype(BF16), wo_ref[...], preferred_element_type=F32)
    hm = _rms(x, nm_ref[...]).astype(BF16)
    for c in range(D_FF // FF_TILE):
        cs = slice(c * FF_TILE, (c + 1) * FF_TILE)
        up = jnp.maximum(jnp.dot(hm, wu_ref[:, cs], preferred_element_type=F32), 0.0)
        x = x + jnp.dot((up * up).astype(BF16), wdn_ref[cs, :], preferred_element_type=F32)
    o_ref[...] = _rms(x, nf_ref[...]) if final_norm else x


def _mix(x, oa, ob, oc, od, proj, wb, wo, nm, wu, wdn, nf, final_norm):
    n = x.shape[0]
    tm = min(n, 512)
    row = lambda w: pl.BlockSpec((tm, w), lambda i: (i, 0))
    once = pl.Buffered(1)
    full2 = lambda shape: pl.BlockSpec(shape, lambda i: (0, 0), pipeline_mode=once)
    return pl.pallas_call(
        functools.partial(_mix_kernel, final_norm=final_norm),
        grid=(n // tm,),
        in_specs=[row(D_MODEL), row(D_BR), row(D_BR), row(D_BR), row(D_BR),
                  pl.BlockSpec((tm, N_GATE), lambda i: (i, COL_GATE // N_GATE)),
                  pl.BlockSpec((N_BRANCH, D_BR, D_MODEL), lambda i: (0, 0, 0), pipeline_mode=once),
                  full2((D_MODEL, D_MODEL)), full2((1, D_MODEL)), full2((D_MODEL, D_FF)),
                  full2((D_FF, D_MODEL)), full2((1, D_MODEL))],
        out_specs=row(D_MODEL),
        out_shape=jax.ShapeDtypeStruct((n, D_MODEL), F32),
        compiler_params=_cparams(("parallel",)),
        name="mix",
    )(x, oa, ob, oc, od, proj, wb, wo, nm, wu, wdn, nf)


def _tiles(b, t):
    tt = min(t, 512)
    bblk = max(1, min(b, 64 // tt)) if tt < 64 else 1
    return bblk, tt


def _layer(x3, st, lw, final_norm):
    (w_in, wb, wo, lb, n_hgrn, bias_c, bias_r, n_mlstm, wc, wd, bd, lg, lbias, n_mix, n_mlp, wu, wdn, nf) = lw
    s_hgrn, s_c, s_n, s_m, buf_c, buf_d = st
    b, t, _ = x3.shape
    L = math.gcd(t, CHUNK)
    bblk, tt = _tiles(b, t)
    x = x3.reshape(b * t, D_MODEL)
    proj = _proj(x, n_mix, w_in)
    proj3 = proj.reshape(b, t, D_PROJ)
    small = proj3[:, :, COL_SMALL:COL_SMALL + 2 * N_HEAD]
    small_row = jnp.swapaxes(small.reshape(b, t // L, L, 2 * N_HEAD), 2, 3)

    oa, s_hgrn_new = _hgrn(proj3, lb, n_hgrn, s_hgrn, L, bblk, tt)
    ob, c_new, n_new, m_new = _mlstm(proj3, small_row, bias_c, bias_r, n_mlstm,
                                     s_c, s_n[:, :, None, :], s_m[:, :, None, None], L, bblk, tt)
    oc, od, buf_c_new, buf_d_new = _conv(proj3, buf_c, buf_d, wc, wd, bd, lg, lbias, bblk, tt)
    flat = lambda a: a.reshape(b * t, D_BR)
    y = _mix(x, flat(oa), flat(ob), flat(oc), flat(od), proj, wb, wo, n_mlp, wu, wdn, nf, final_norm)
    return y.reshape(b, t, D_MODEL), (s_hgrn_new, c_new, n_new[:, :, 0, :], m_new[:, :, 0, 0],
                                      buf_c_new, buf_d_new)


def kernel(x_prompt, x_sample, state_hgrn, state_mlstm_c, state_mlstm_n, state_mlstm_m, state_conv_short, state_conv_conformer, w_in, w_branch, w_out, lb_logits, norm_hgrn, mlstm_i_bias, mlstm_f_bias, norm_mlstm, conv_short_w, conv_conformer_w, conv_conformer_b, ln_conformer_g, ln_conformer_b, norm_mix, norm_mlp, w_up, w_down, norm_final):
    depth = w_in.shape[0]
    bp = x_prompt.shape[0]
    lower_bounds = _lower_bounds(lb_logits)

    o_small = N_MAIN
    o_conv = N_MAIN + 2 * N_HEAD
    o_gate = o_conv + N_CONV
    w_in_p = jnp.concatenate(
        [w_in[:, :, :N_MAIN], w_in[:, :, o_gate:o_gate + N_GATE], w_in[:, :, o_conv:o_conv + N_CONV],
         w_in[:, :, o_small:o_small + 2 * N_HEAD],
         jnp.zeros((depth, D_MODEL, N_SMALL - 2 * N_HEAD), w_in.dtype)], axis=2).astype(BF16)
    wb16, wo16, wu16, wdn16 = (a.astype(BF16) for a in (w_branch, w_out, w_up, w_down))
    gate_bias = jnp.concatenate([mlstm_i_bias, mlstm_f_bias], axis=1).astype(F32)
    bias_c = jnp.pad(gate_bias, ((0, 0), (0, N_SMALL - 2 * N_HEAD)))[:, None, :]
    bias_r = gate_bias[:, :, None]
    row = lambda a, l: a[l][None, :].astype(F32)

    xp, xs = x_prompt, x_sample
    new_p, new_s = [], []
    for l in range(depth):
        last = l == depth - 1
        lw = (w_in_p[l], wb16[l], wo16[l], lower_bounds[l][None, :], row(norm_hgrn, l), bias_c[l], bias_r[l],
              row(norm_mlstm, l), conv_short_w[l], conv_conformer_w[l], row(conv_conformer_b, l),
              row(ln_conformer_g, l), row(ln_conformer_b, l), row(norm_mix, l), row(norm_mlp, l),
              wu16[l], wdn16[l], norm_final[None, :].astype(F32))
        zero_st = (jnp.zeros((bp, N_HEAD, D_HEAD, D_HEAD), F32), jnp.zeros((bp, N_HEAD, D_HEAD, D_HEAD), F32),
                   jnp.zeros((bp, N_HEAD, D_HEAD), F32), jnp.zeros((bp, N_HEAD), F32),
                   jnp.zeros((bp, CONV_C - 1, D_BR), F32), jnp.zeros((bp, CONV_D - 1, D_BR), F32))
        past_st = (state_hgrn[l], state_mlstm_c[l], state_mlstm_n[l], state_mlstm_m[l],
                   state_conv_short[l], state_conv_conformer[l])
        xp, st_p = _layer(xp, zero_st, lw, last)
        xs, st_s = _layer(xs, past_st, lw, last)
        new_p.append(st_p)
        new_s.append(st_s)

    outs = [xp, xs]
    for i in range(6):
        outs.append(jnp.stack([s[i] for s in new_p]))
        outs.append(jnp.stack([s[i] for s in new_s]))
    return tuple(outs)
```

```python
import functools
import math

import jax
import jax.numpy as jnp
from jax import lax
from jax.experimental import pallas as pl
from jax.experimental.pallas import tpu as pltpu

F32 = jnp.float32
BF16 = jnp.bfloat16

D_MODEL = 1024
N_HEAD = 4
D_HEAD = 128
D_BR = 512
N_BRANCH = 4
CONV_C = 3
CONV_D = 31
D_FF = 4 * D_MODEL
CHUNK = 64
EPS = 1e-6
SUBLANES = 8
LANES = 128

N_MAIN = 8 * D_BR
N_GATE = N_BRANCH * D_MODEL
N_CONV = 5 * D_BR
N_SMALL = LANES
D_PROJ = N_MAIN + N_GATE + N_CONV + N_SMALL
COL_GATE = N_MAIN
COL_CONV = N_MAIN + N_GATE
COL_SMALL = N_MAIN + N_GATE + N_CONV
PROJ_TN = D_PROJ // 5

VMEM_LIMIT = 56 * 1024 * 1024
ROW_TILE = 512
SEQ_GROUP = 8


def _cparams(sem, vmem_limit=VMEM_LIMIT):
    return pltpu.CompilerParams(dimension_semantics=sem, vmem_limit_bytes=vmem_limit)


def _dot(a, b):
    return jnp.dot(a.astype(BF16), b.astype(BF16), preferred_element_type=F32)


def _dot_nt(a, b):
    return lax.dot_general(a.astype(BF16), b.astype(BF16), (((1,), (1,)), ((), ())),
                           preferred_element_type=F32)


def _dot_tn(a, b):
    return lax.dot_general(a.astype(BF16), b.astype(BF16), (((0,), (0,)), ((), ())),
                           preferred_element_type=F32)


def _split3(x):
    hi = x.astype(BF16)
    r = x - hi.astype(F32)
    mid = r.astype(BF16)
    lo = (r - mid.astype(F32)).astype(BF16)
    return hi, mid, lo


def _sigmoid(x):
    return jax.nn.sigmoid(x)


def _log_sigmoid(x):
    return jnp.minimum(x, 0.0) - jnp.log1p(jnp.exp(-jnp.abs(x)))


def _rms(x, g):
    return x * lax.rsqrt(jnp.mean(x * x, axis=-1, keepdims=True) + EPS) * g


def _layer_block(shape, l, grid_rank):
    zeros = (0,) * (len(shape) - 1)
    if grid_rank == 2:
        return pl.BlockSpec((None,) + shape, lambda i, j: (l, i) + zeros)
    return pl.BlockSpec((None,) + shape, lambda i: (l, i) + zeros)


def _lb_kernel(x_ref, o_ref):
    x = x_ref[...]
    depth = x.shape[0]
    e = jnp.exp(x - jnp.max(x, axis=0, keepdims=True))
    s = e / jnp.sum(e, axis=0, keepdims=True)
    acc = jnp.zeros_like(s[0:1])
    for i in range(depth):
        acc = acc + s[i:i + 1]
        o_ref[i:i + 1, :] = acc - s[0:1]


def _lower_bounds(lb_logits):
    return pl.pallas_call(
        _lb_kernel, out_shape=jax.ShapeDtypeStruct(lb_logits.shape, F32), name="lower_bounds",
    )(lb_logits.astype(F32))


def _norm_kernel(x_ref, g_ref, o_ref):
    o_ref[...] = _rms(x_ref[...], g_ref[...]).astype(o_ref.dtype)


def _norm(x, g):
    n = x.shape[0]
    tm = min(n, ROW_TILE)
    return pl.pallas_call(
        _norm_kernel,
        grid=(n // tm,),
        in_specs=[pl.BlockSpec((tm, D_MODEL), lambda i: (i, 0)), pl.BlockSpec((1, D_MODEL), lambda i: (0, 0))],
        out_specs=pl.BlockSpec((tm, D_MODEL), lambda i: (i, 0)),
        out_shape=jax.ShapeDtypeStruct((n, D_MODEL), BF16),
        compiler_params=_cparams(("parallel",)),
        name="norm",
    )(x, g)


def _proj_kernel(h_ref, w_ref, o_ref):
    o_ref[...] = jnp.dot(h_ref[...], w_ref[...], preferred_element_type=F32)


def _proj(h, w, l):
    n = h.shape[0]
    tm = min(n, ROW_TILE)
    return pl.pallas_call(
        _proj_kernel,
        grid=(D_PROJ // PROJ_TN, n // tm),
        in_specs=[pl.BlockSpec((tm, D_MODEL), lambda j, i: (i, 0)),
                  pl.BlockSpec((None, D_MODEL, PROJ_TN), lambda j, i: (l, 0, j))],
        out_specs=pl.BlockSpec((tm, PROJ_TN), lambda j, i: (i, j)),
        out_shape=jax.ShapeDtypeStruct((n, D_PROJ), F32),
        compiler_params=_cparams(("parallel", "parallel")),
        name="proj",
    )(h, w)


def _cumsum_rows(x):
    n = x.shape[0]
    row = lax.broadcasted_iota(jnp.int32, x.shape, 0)
    s = 1
    while s < n:
        x = x + jnp.where(row >= s, pltpu.roll(x, s, 0), 0.0)
        s *= 2
    return x


def _group_row(x, group, j):
    n, w = x.shape
    x3 = x.reshape(n // group, group, w)
    return jnp.broadcast_to(x3[:, j:j + 1, :], x3.shape).reshape(n, w)


def _boundary_row(x, c, row):
    if 2 * c >= SUBLANES:
        return _group_row(x, 2 * c, c - 1)
    if c == 2:
        return jnp.where(row % SUBLANES < 4, _group_row(x, SUBLANES, 1), _group_row(x, SUBLANES, 5))
    return jnp.where(row % 2 == 1, pltpu.roll(x, 1, 0), x)


def _hgrn_kernel(qa_ref, fa_ref, ia_ref, ga_ref, lb_ref, nw_ref, s0_ref, prev_ref, o_ref, s_ref, *, L):
    del prev_ref
    G, tt, _ = qa_ref.shape
    nchunk = tt // L
    levels = [c for c in (32, 16, 8, 4, 2, 1) if 2 * c <= L]

    @pl.when(pl.program_id(1) == 0)
    def _():
        s_ref[...] = s0_ref[...]

    lb = lb_ref[...]
    nw = nw_ref[...]
    row = lax.broadcasted_iota(jnp.int32, (L, D_BR), 0)
    row_l = lax.broadcasted_iota(jnp.int32, (L, L), 0)
    col_l = lax.broadcasted_iota(jnp.int32, (L, L), 1)
    heads = [slice(h * D_HEAD, (h + 1) * D_HEAD) for h in range(N_HEAD)]

    def one_sequence(bi, rows):
        f = lb + (1.0 - lb) * _sigmoid(fa_ref[bi, rows, :])
        g = jnp.log(f)
        k = 1.0 - f
        qa = qa_ref[bi, rows, :]
        q = qa * _sigmoid(qa)
        v = ia_ref[bi, rows, :]
        gc = _cumsum_rows(g)

        qg = q * jnp.exp(gc)
        out = [_dot(qg[:, hs], s_ref[bi, h]) for h, hs in enumerate(heads)]

        scores = [jnp.where(row_l == col_l, _dot_nt(q[:, hs], k[:, hs]), 0.0) for hs in heads]
        for c in levels:
            diff = gc - _boundary_row(gc, c, row)
            qe = q * jnp.exp(diff)
            ke = k * jnp.exp(-diff)
            valid = ((row_l // (2 * c)) == (col_l // (2 * c))) & (row_l % (2 * c) >= c) & (col_l % (2 * c) < c)
            for h, hs in enumerate(heads):
                scores[h] = scores[h] + jnp.where(valid, _dot_nt(qe[:, hs], ke[:, hs]), 0.0)

        g_last = gc[L - 1:L, :]
        k_dec = k * jnp.exp(g_last - gc)
        decay = jnp.exp(g_last)
        ga = ga_ref[bi, rows, :]
        gate = ga * _sigmoid(ga)
        for h, hs in enumerate(heads):
            o = out[h] + _dot(scores[h], v[:, hs])
            decay_col = jnp.broadcast_to(decay[:, hs], (D_HEAD, D_HEAD)).T
            s_ref[bi, h] = decay_col * s_ref[bi, h] + _dot_tn(k_dec[:, hs], v[:, hs])
            y = o * lax.rsqrt(jnp.mean(o * o, axis=-1, keepdims=True) + EPS) * nw[:, hs]
            o_ref[bi, rows, hs] = (y * gate[:, hs]).astype(o_ref.dtype)

    def body(ci, carry):
        rows = pl.ds(pl.multiple_of(ci * L, L), L)
        for bi in range(G):
            one_sequence(bi, rows)
        return carry

    lax.fori_loop(0, nchunk, body, 0)


def _hgrn(proj3, lb, nw, s_in, l_in, s_prev, l, depth, L, G, tt):
    b, t, _ = proj3.shape
    col = lambda c: pl.BlockSpec((G, tt, D_BR), lambda i, j: (i, j, c))
    vec = pl.BlockSpec((1, D_BR), lambda i, j: (0, 0))
    st = (G, N_HEAD, D_HEAD, D_HEAD)
    aliases = {}
    if s_prev is None:
        s_prev = jnp.zeros((1,), F32)
    else:
        aliases = {7: 1}
    return pl.pallas_call(
        functools.partial(_hgrn_kernel, L=L),
        grid=(b // G, t // tt),
        in_specs=[col(0), col(1), col(2), col(3), vec, vec, _layer_block(st, l_in, 2),
                  pl.BlockSpec(memory_space=pl.ANY)],
        out_specs=[pl.BlockSpec((G, tt, D_BR), lambda i, j: (i, j, 0)), _layer_block(st, l, 2)],
        out_shape=[jax.ShapeDtypeStruct((b, t, D_BR), BF16),
                   jax.ShapeDtypeStruct((depth, b) + st[1:], F32)],
        input_output_aliases=aliases,
        compiler_params=_cparams(("parallel", "arbitrary")),
        name="hgrn",
    )(proj3, proj3, proj3, proj3, lb, nw, s_in, s_prev)


def _mlstm_kernel(q_ref, k_ref, v_ref, og_ref, sc_ref, sr_ref, bc_ref, br_ref, nw_ref,
                  c0_ref, n0_ref, m0_ref, cp_ref, np_ref, mp_ref, o_ref, c_ref, n_ref, m_ref, *, L):
    del cp_ref, np_ref, mp_ref
    G, tt, _ = q_ref.shape
    nchunk = tt // L

    @pl.when(pl.program_id(1) == 0)
    def _():
        c_ref[...] = c0_ref[...]
        n_ref[...] = n0_ref[...]
        m_ref[...] = m0_ref[...]

    nw = nw_ref[...]
    bias_c = bc_ref[...]
    bias_r = br_ref[...]
    row_l = lax.broadcasted_iota(jnp.int32, (L, L), 0)
    col_l = lax.broadcasted_iota(jnp.int32, (L, L), 1)
    causal = row_l >= col_l
    tri = jnp.where(causal, 1.0, 0.0).astype(BF16)
    tri_t = jnp.where(row_l <= col_l, 1.0, 0.0).astype(BF16)
    row_c = lax.broadcasted_iota(jnp.int32, (L, N_SMALL), 0)
    ones = jnp.ones((L, D_HEAD), F32)
    scale = D_HEAD ** -0.5
    heads = [slice(h * D_HEAD, (h + 1) * D_HEAD) for h in range(N_HEAD)]
    group = G if L < CHUNK else min(G, 2)

    def gates(bi, ci, rows):
        pre_c = sc_ref[bi, rows, :] + bias_c
        pre_r = sr_ref[bi, ci] + bias_r
        hi, mid, lo = _split3(_log_sigmoid(pre_c))
        b_c = (jnp.dot(tri, hi, preferred_element_type=F32) + jnp.dot(tri, mid, preferred_element_type=F32)
               + jnp.dot(tri, lo, preferred_element_type=F32))
        hi, mid, lo = _split3(_log_sigmoid(pre_r))
        b_r = (jnp.dot(hi, tri_t, preferred_element_type=F32) + jnp.dot(mid, tri_t, preferred_element_type=F32)
               + jnp.dot(lo, tri_t, preferred_element_type=F32))
        b = pltpu.roll(b_c, N_SMALL - N_HEAD, 1)
        a = pre_c - b
        cm = a
        s = 1
        while s < L:
            cm = jnp.maximum(cm, jnp.where(row_c >= s, pltpu.roll(cm, s, 0), -jnp.inf))
            s *= 2
        m_old = m_ref[bi]
        big_m = jnp.maximum(m_old, cm)
        m_last = big_m[L - 1:L, :]
        m_ref[bi] = b[L - 1:L, :] + m_last
        return dict(a_r=pre_r[0:N_HEAD, :] - b_r[N_HEAD:2 * N_HEAD, :], big_m=big_m, m_old=m_old,
                    w_inter=jnp.exp(m_old - big_m), n_exp=jnp.exp(-(b + big_m)),
                    w_state=jnp.exp(a - m_last), carry=jnp.exp(m_old - m_last))

    def body(ci, carry):
        rows = pl.ds(pl.multiple_of(ci * L, L), L)
        for g0 in range(0, G, group):
            seqs = range(g0, g0 + group)
            sh = {bi: gates(bi, ci, rows) for bi in seqs}
            probs = [(bi, h) for bi in seqs for h in range(N_HEAD)]
            qk, wi_b, intra, qc, qn = {}, {}, {}, {}, {}
            for p in probs:
                bi, h = p
                g = sh[bi]
                m_b = jnp.broadcast_to(g["big_m"][:, h:h + 1], (L, D_HEAD))
                wi_b[p] = jnp.exp(g["m_old"][:, h:h + 1] - m_b)
                w = jnp.where(causal, jnp.exp(g["a_r"][h:h + 1, :] - m_b[:, :L]), 0.0)
                q = q_ref[bi, rows, heads[h]] * scale
                qk[p] = _dot_nt(q, k_ref[bi, rows, heads[h]]) * w
            for p in probs:
                bi, h = p
                q = q_ref[bi, rows, heads[h]] * scale
                intra[p] = _dot(qk[p], jnp.concatenate([v_ref[bi, rows, heads[h]], ones], axis=1))
                qc[p] = _dot(q, c_ref[bi, h])
                qn[p] = _dot_nt(q, jnp.broadcast_to(n_ref[bi, h], (D_HEAD, D_HEAD)))
            for p in probs:
                bi, h = p
                g = sh[bi]
                num = intra[p][:, :D_HEAD] + wi_b[p] * qc[p]
                den = intra[p][:, D_HEAD:D_HEAD + 1] + g["w_inter"][:, h:h + 1] * qn[p][:, 0:1]
                r = 1.0 / jnp.maximum(jnp.abs(den), g["n_exp"][:, h:h + 1])
                t = r * lax.rsqrt(r * r * jnp.mean(num * num, axis=-1, keepdims=True) + EPS)
                gate = _sigmoid(og_ref[bi, rows, heads[h]])
                o_ref[bi, rows, heads[h]] = (num * t * nw[:, heads[h]] * gate).astype(o_ref.dtype)
            for p in probs:
                bi, h = p
                g = sh[bi]
                k = k_ref[bi, rows, heads[h]]
                kw = g["w_state"][:, h:h + 1] * k
                cd = g["carry"][:, h:h + 1]
                c_ref[bi, h] = cd * c_ref[bi, h] + _dot_tn(kw, v_ref[bi, rows, heads[h]])
                n_ref[bi, h] = cd * n_ref[bi, h] + jnp.sum(kw, axis=0, keepdims=True)
        return carry

    lax.fori_loop(0, nchunk, body, 0)


def _mlstm(proj3, small_row, bias_c, bias_r, nw, st_in, l_in, st_prev, l, depth, L, G, tt):
    b, t, _ = proj3.shape
    col = lambda c: pl.BlockSpec((G, tt, D_BR), lambda i, j: (i, j, c))
    full2 = lambda shape: pl.BlockSpec(shape, lambda i, j: (0, 0))
    shapes = [(G, N_HEAD, D_HEAD, D_HEAD), (G, N_HEAD, 1, D_HEAD), (G, 1, N_SMALL)]
    aliases = {}
    if st_prev is None:
        st_prev = [jnp.zeros((1,), F32)] * 3
    else:
        aliases = {12: 1, 13: 2, 14: 3}
    any_spec = pl.BlockSpec(memory_space=pl.ANY)
    return pl.pallas_call(
        functools.partial(_mlstm_kernel, L=L),
        grid=(b // G, t // tt),
        in_specs=[col(4), col(5), col(6), col(7),
                  pl.BlockSpec((G, tt, N_SMALL), lambda i, j: (i, j, COL_SMALL // N_SMALL)),
                  pl.BlockSpec((G, tt // L, 2 * N_HEAD, L), lambda i, j: (i, j, 0, 0)),
                  full2((1, N_SMALL)), full2((2 * N_HEAD, 1)), full2((1, D_BR))]
                 + [_layer_block(s, l_in, 2) for s in shapes] + [any_spec] * 3,
        out_specs=[pl.BlockSpec((G, tt, D_BR), lambda i, j: (i, j, 0))] + [_layer_block(s, l, 2) for s in shapes],
        out_shape=[jax.ShapeDtypeStruct((b, t, D_BR), BF16)]
                  + [jax.ShapeDtypeStruct((depth, b) + s[1:], F32) for s in shapes],
        input_output_aliases=aliases,
        compiler_params=_cparams(("parallel", "arbitrary")),
        name="mlstm",
    )(proj3, proj3, proj3, proj3, proj3, small_row, bias_c, bias_r, nw, *st_in, *st_prev)


PAD_C = 8
PAD_D = 32
CONV_ROWS = 32


def _conv_kernel(bc_ref, cc_ref, xc_ref, ad_ref, gd_ref, bufc_ref, bufd_ref, wc_ref, wd_ref, bd_ref,
                 lg_ref, lbias_ref, pc_ref, pd_ref, oc_ref, od_ref, nbc_ref, nbd_ref, extc, extd):
    del pc_ref, pd_ref
    bblk, tt, _ = bc_ref.shape
    hc, hd = CONV_C - 1, CONV_D - 1
    rt = min(tt, CONV_ROWS)

    @pl.when(pl.program_id(1) == 0)
    def _():
        extc[:, PAD_C - hc:PAD_C, :] = bufc_ref[...]
        extd[:, PAD_D - hd:PAD_D, :] = bufd_ref[...]

    extc[:, PAD_C:PAD_C + tt, :] = cc_ref[...] * xc_ref[...]
    extd[:, PAD_D:PAD_D + tt, :] = ad_ref[...] * _sigmoid(gd_ref[...])

    def body(idx, carry):
        bi = idx // (tt // rt)
        r0 = pl.multiple_of((idx % (tt // rt)) * rt, rt)
        win = extc[bi, pl.ds(r0, rt + PAD_C), :]
        acc = jnp.zeros((rt, D_BR), F32)
        for j in range(CONV_C):
            off = PAD_C - hc + j
            acc = acc + wc_ref[j:j + 1, :] * win[off:off + rt]
        oc_ref[bi, pl.ds(r0, rt), :] = (bc_ref[bi, pl.ds(r0, rt), :] * acc).astype(oc_ref.dtype)
        win = extd[bi, pl.ds(r0, rt + PAD_D), :]
        acc = jnp.zeros((rt, D_BR), F32)
        for res in range(SUBLANES):
            taps = [j for j in range(CONV_D) if (PAD_D - hd + j) % SUBLANES == res]
            span = max(PAD_D - hd + j for j in taps) - res + rt
            shifted = win[res:res + span]
            for j in taps:
                off = PAD_D - hd + j - res
                acc = acc + wd_ref[j:j + 1, :] * shifted[off:off + rt]
        y = acc + bd_ref[...]
        yc = y - jnp.mean(y, axis=-1, keepdims=True)
        z = yc * lax.rsqrt(jnp.mean(yc * yc, axis=-1, keepdims=True) + EPS) * lg_ref[...] + lbias_ref[...]
        od_ref[bi, pl.ds(r0, rt), :] = (z * _sigmoid(z)).astype(od_ref.dtype)
        return carry

    lax.fori_loop(0, bblk * (tt // rt), body, 0)

    new_c = extc[:, PAD_C + tt - hc:PAD_C + tt, :]
    new_d = extd[:, PAD_D + tt - hd:PAD_D + tt, :]
    nbc_ref[...] = new_c
    nbd_ref[...] = new_d
    extc[:, PAD_C - hc:PAD_C, :] = new_c
    extd[:, PAD_D - hd:PAD_D, :] = new_d


def _conv(proj3, buf_in, l_in, buf_prev, l, depth, wc, wd, bd, lg, lbias, bblk, tt):
    b, t, _ = proj3.shape
    c0 = COL_CONV // D_BR
    col = lambda c: pl.BlockSpec((bblk, tt, D_BR), lambda i, j: (i, j, c0 + c))
    full2 = lambda shape: pl.BlockSpec(shape, lambda i, j: (0, 0))
    shapes = [(bblk, CONV_C - 1, D_BR), (bblk, CONV_D - 1, D_BR)]
    outc = pl.BlockSpec((bblk, tt, D_BR), lambda i, j: (i, j, 0))
    aliases = {}
    if buf_prev is None:
        buf_prev = [jnp.zeros((1,), F32)] * 2
    else:
        aliases = {12: 2, 13: 3}
    any_spec = pl.BlockSpec(memory_space=pl.ANY)
    return pl.pallas_call(
        _conv_kernel,
        grid=(b // bblk, t // tt),
        in_specs=[col(0), col(1), col(2), col(3), col(4)] + [_layer_block(s, l_in, 2) for s in shapes]
                 + [full2((CONV_C, D_BR)), full2((CONV_D, D_BR)), full2((1, D_BR)), full2((1, D_BR)),
                    full2((1, D_BR)), any_spec, any_spec],
        out_specs=[outc, outc] + [_layer_block(s, l, 2) for s in shapes],
        out_shape=[jax.ShapeDtypeStruct((b, t, D_BR), BF16), jax.ShapeDtypeStruct((b, t, D_BR), BF16)]
                  + [jax.ShapeDtypeStruct((depth, b) + s[1:], F32) for s in shapes],
        scratch_shapes=[pltpu.VMEM((bblk, PAD_C + tt, D_BR), F32), pltpu.VMEM((bblk, PAD_D + tt, D_BR), F32)],
        input_output_aliases=aliases,
        compiler_params=_cparams(("parallel", "arbitrary")),
        name="conv",
    )(proj3, proj3, proj3, proj3, proj3, *buf_in, wc, wd, bd, lg, lbias, *buf_prev)


FF_TILE = 1024
MIX_VMEM_LIMIT = 60 * 1024 * 1024


def _mix_kernel(x_ref, oa_ref, ob_ref, oc_ref, od_ref, gt_ref, wb_ref, wo_ref, nm_ref, wu_ref, wdn_ref,
                nn_ref, *out_refs, last):
    merged = None
    for n, br in enumerate((oa_ref, ob_ref, oc_ref, od_ref)):
        gate = _sigmoid(gt_ref[:, n * D_MODEL:(n + 1) * D_MODEL])
        term = gate * jnp.dot(br[...], wb_ref[n], preferred_element_type=F32)
        merged = term if merged is None else merged + term
    x = x_ref[...] + jnp.dot(merged.astype(BF16), wo_ref[...], preferred_element_type=F32)
    hm = _rms(x, nm_ref[...]).astype(BF16)
    for c in range(D_FF // FF_TILE):
        cs = slice(c * FF_TILE, (c + 1) * FF_TILE)
        up = jnp.maximum(jnp.dot(hm, wu_ref[:, cs], preferred_element_type=F32), 0.0)
        x = x + jnp.dot((up * up).astype(BF16), wdn_ref[cs, :], preferred_element_type=F32)
    if last:
        out_refs[0][...] = _rms(x, nn_ref[...])
    else:
        out_refs[0][...] = x
        out_refs[1][...] = _rms(x, nn_ref[...]).astype(BF16)


def _mix(x, oa, ob, oc, od, proj, wb, wo, nm, wu, wdn, nn, l, last):
    n = x.shape[0]
    tm = min(n, ROW_TILE)
    row = lambda w: pl.BlockSpec((tm, w), lambda i: (i, 0))
    once = pl.Buffered(1)
    full2 = lambda shape: pl.BlockSpec(shape, lambda i: (0, 0))
    layer3 = lambda shape: pl.BlockSpec((None,) + shape, lambda i: (l, 0, 0), pipeline_mode=once)
    out_specs = [row(D_MODEL)] if last else [row(D_MODEL), row(D_MODEL)]
    out_shape = [jax.ShapeDtypeStruct((n, D_MODEL), F32)]
    if not last:
        out_shape.append(jax.ShapeDtypeStruct((n, D_MODEL), BF16))
    return pl.pallas_call(
        functools.partial(_mix_kernel, last=last),
        grid=(n // tm,),
        in_specs=[row(D_MODEL), row(D_BR), row(D_BR), row(D_BR), row(D_BR),
                  pl.BlockSpec((tm, N_GATE), lambda i: (i, COL_GATE // N_GATE)),
                  pl.BlockSpec((None, N_BRANCH, D_BR, D_MODEL), lambda i: (l, 0, 0, 0), pipeline_mode=once),
                  layer3((D_MODEL, D_MODEL)), full2((1, D_MODEL)), layer3((D_MODEL, D_FF)),
                  layer3((D_FF, D_MODEL)), full2((1, D_MODEL))],
        out_specs=out_specs,
        out_shape=out_shape,
        compiler_params=_cparams(("parallel",), MIX_VMEM_LIMIT),
        name="mix",
    )(x, oa, ob, oc, od, proj, wb, wo, nm, wu, wdn, nn)


def _layer(x, h, b, t, st_in, l_in, st_prev, l, depth, lw, wts, last):
    (lb, n_hgrn, bias_c, bias_r, n_mlstm, wc, wd, bd, lg, lbias, n_mlp, n_next) = lw
    w_in, wb, wo, wu, wdn = wts
    L = math.gcd(t, CHUNK)
    G = min(b, SEQ_GROUP)
    proj = _proj(h, w_in, l)
    proj3 = proj.reshape(b, t, D_PROJ)
    small = proj3[:, :, COL_SMALL:COL_SMALL + 2 * N_HEAD]
    small_row = jnp.swapaxes(small.reshape(b, t // L, L, 2 * N_HEAD), 2, 3)

    prev = (None,) * 6 if st_prev is None else st_prev
    g_h, tt_h = (2, 512) if t >= 512 else (G, t)
    oa, s_hgrn = _hgrn(proj3, lb, n_hgrn, st_in[0], l_in, prev[0], l, depth, L, g_h, tt_h)
    tt_m = min(t, 128)
    ob, s_c, s_n, s_m = _mlstm(proj3, small_row, bias_c, bias_r, n_mlstm, st_in[1:4], l_in,
                               None if st_prev is None else prev[1:4], l, depth, L, G, tt_m)
    g_c, tt_c = (1, 512) if t >= 512 else (G, t)
    oc, od, buf_c, buf_d = _conv(proj3, st_in[4:6], l_in, None if st_prev is None else prev[4:6], l, depth,
                                 wc, wd, bd, lg, lbias, g_c, tt_c)
    flat = lambda a: a.reshape(b * t, D_BR)
    outs = _mix(x, flat(oa), flat(ob), flat(oc), flat(od), proj, wb, wo, n_mlp, wu, wdn, n_next, l, last)
    return outs, (s_hgrn, s_c, s_n, s_m, buf_c, buf_d)


def kernel(x_prompt, x_sample, state_hgrn, state_mlstm_c, state_mlstm_n, state_mlstm_m, state_conv_short, state_conv_conformer, w_in, w_branch, w_out, lb_logits, norm_hgrn, mlstm_i_bias, mlstm_f_bias, norm_mlstm, conv_short_w, conv_conformer_w, conv_conformer_b, ln_conformer_g, ln_conformer_b, norm_mix, norm_mlp, w_up, w_down, norm_final):
    depth = w_in.shape[0]
    bp, tp, _ = x_prompt.shape
    bs, ts, _ = x_sample.shape
    lower_bounds = _lower_bounds(lb_logits)

    o_small = N_MAIN
    o_conv = N_MAIN + 2 * N_HEAD
    o_gate = o_conv + N_CONV
    w_in_p = jnp.concatenate(
        [w_in[:, :, :N_MAIN], w_in[:, :, o_gate:o_gate + N_GATE], w_in[:, :, o_conv:o_conv + N_CONV],
         w_in[:, :, o_small:o_small + 2 * N_HEAD],
         jnp.zeros((depth, D_MODEL, N_SMALL - 2 * N_HEAD), w_in.dtype)], axis=2).astype(BF16)
    wts = (w_in_p,) + tuple(a.astype(BF16) for a in (w_branch, w_out, w_up, w_down))
    gate_bias = jnp.concatenate([mlstm_i_bias, mlstm_f_bias], axis=1).astype(F32)
    bias_c = jnp.pad(gate_bias, ((0, 0), (0, N_SMALL - 2 * N_HEAD)))[:, None, :]
    bias_r = gate_bias[:, :, None]
    row = lambda a, l: a[l][None, :].astype(F32)

    zero_st = (jnp.zeros((1, bp, N_HEAD, D_HEAD, D_HEAD), F32), jnp.zeros((1, bp, N_HEAD, D_HEAD, D_HEAD), F32),
               jnp.zeros((1, bp, N_HEAD, 1, D_HEAD), F32), jnp.zeros((1, bp, 1, N_SMALL), F32),
               jnp.zeros((1, bp, CONV_C - 1, D_BR), F32), jnp.zeros((1, bp, CONV_D - 1, D_BR), F32))
    m_lanes = jnp.pad(state_mlstm_m.astype(F32), ((0, 0), (0, 0), (0, N_SMALL - N_HEAD)))[:, :, None, :]
    past_st = (state_hgrn, state_mlstm_c, state_mlstm_n[:, :, :, None, :], m_lanes,
               state_conv_short, state_conv_conformer)

    xp = x_prompt.reshape(bp * tp, D_MODEL)
    xs = x_sample.reshape(bs * ts, D_MODEL)
    hp = _norm(xp, row(norm_mix, 0))
    hs = _norm(xs, row(norm_mix, 0))
    st_p = st_s = None
    for l in range(depth):
        last = l == depth - 1
        n_next = norm_final[None, :].astype(F32) if last else row(norm_mix, l + 1)
        lw = (lower_bounds[l][None, :], row(norm_hgrn, l), bias_c[l], bias_r[l], row(norm_mlstm, l),
              conv_short_w[l], conv_conformer_w[l], row(conv_conformer_b, l), row(ln_conformer_g, l),
              row(ln_conformer_b, l), row(norm_mlp, l), n_next)
        outs_p, st_p = _layer(xp, hp, bp, tp, zero_st, 0, st_p, l, depth, lw, wts, last)
        outs_s, st_s = _layer(xs, hs, bs, ts, past_st, l, st_s, l, depth, lw, wts, last)
        if last:
            xp, xs = outs_p[0], outs_s[0]
        else:
            (xp, hp), (xs, hs) = outs_p, outs_s

    def finish(x, b, t, st):
        s_hgrn, s_c, s_n, s_m, buf_c, buf_d = st
        return (x.reshape(b, t, D_MODEL), s_hgrn, s_c, s_n[:, :, :, 0, :], s_m[:, :, 0, :N_HEAD], buf_c, buf_d)

    fp = finish(xp, bp, tp, st_p)
    fs = finish(xs, bs, ts, st_s)
    outs = []
    for a, c in zip(fp, fs):
        outs += [a, c]
    return tuple(outs)
```

```python
import functools
import math

import jax
import jax.numpy as jnp
from jax import lax
from jax.experimental import pallas as pl
from jax.experimental.pallas import tpu as pltpu

F32 = jnp.float32
BF16 = jnp.bfloat16

D_MODEL = 1024
N_HEAD = 4
D_HEAD = 128
D_BR = 512
N_BRANCH = 4
CONV_C = 3
CONV_D = 31
D_FF = 4 * D_MODEL
CHUNK = 64
EPS = 1e-6
SUBLANES = 8
LANES = 128

N_MAIN = 8 * D_BR
N_GATE = N_BRANCH * D_MODEL
N_CONV = 5 * D_BR
N_SMALL = LANES
D_PROJ = N_MAIN + N_GATE + N_CONV + N_SMALL
COL_GATE = N_MAIN
COL_CONV = N_MAIN + N_GATE
COL_SMALL = N_MAIN + N_GATE + N_CONV
PROJ_TN = D_PROJ // 5

VMEM_LIMIT = 56 * 1024 * 1024
ROW_TILE = 512
SEQ_GROUP = 8


def _cparams(sem, vmem_limit=VMEM_LIMIT):
    return pltpu.CompilerParams(dimension_semantics=sem, vmem_limit_bytes=vmem_limit)


def _dot(a, b):
    return jnp.dot(a.astype(BF16), b.astype(BF16), preferred_element_type=F32)


def _dot_nt(a, b):
    return lax.dot_general(a.astype(BF16), b.astype(BF16), (((1,), (1,)), ((), ())),
                           preferred_element_type=F32)


def _dot_tn(a, b):
    return lax.dot_general(a.astype(BF16), b.astype(BF16), (((0,), (0,)), ((), ())),
                           preferred_element_type=F32)


def _split3(x):
    hi = x.astype(BF16)
    r = x - hi.astype(F32)
    mid = r.astype(BF16)
    lo = (r - mid.astype(F32)).astype(BF16)
    return hi, mid, lo


def _sigmoid(x):
    return jax.nn.sigmoid(x)


def _log_sigmoid(x):
    return jnp.minimum(x, 0.0) - jnp.log1p(jnp.exp(-jnp.abs(x)))


def _rms(x, g):
    return x * lax.rsqrt(jnp.mean(x * x, axis=-1, keepdims=True) + EPS) * g


def _layer_block(shape, l, grid_rank):
    zeros = (0,) * (len(shape) - 1)
    if grid_rank == 2:
        return pl.BlockSpec((None,) + shape, lambda i, j: (l, i) + zeros)
    return pl.BlockSpec((None,) + shape, lambda i: (l, i) + zeros)


def _lb_kernel(x_ref, o_ref):
    x = x_ref[...]
    depth = x.shape[0]
    e = jnp.exp(x - jnp.max(x, axis=0, keepdims=True))
    s = e / jnp.sum(e, axis=0, keepdims=True)
    acc = jnp.zeros_like(s[0:1])
    for i in range(depth):
        acc = acc + s[i:i + 1]
        o_ref[i:i + 1, :] = acc - s[0:1]


def _lower_bounds(lb_logits):
    return pl.pallas_call(
        _lb_kernel, out_shape=jax.ShapeDtypeStruct(lb_logits.shape, F32), name="lower_bounds",
    )(lb_logits.astype(F32))


def _norm_kernel(x_ref, g_ref, o_ref):
    o_ref[...] = _rms(x_ref[...], g_ref[...]).astype(o_ref.dtype)


def _norm(x, g):
    n = x.shape[0]
    tm = min(n, ROW_TILE)
    return pl.pallas_call(
        _norm_kernel,
        grid=(n // tm,),
        in_specs=[pl.BlockSpec((tm, D_MODEL), lambda i: (i, 0)), pl.BlockSpec((1, D_MODEL), lambda i: (0, 0))],
        out_specs=pl.BlockSpec((tm, D_MODEL), lambda i: (i, 0)),
        out_shape=jax.ShapeDtypeStruct((n, D_MODEL), BF16),
        compiler_params=_cparams(("parallel",)),
        name="norm",
    )(x, g)


def _proj_kernel(h_ref, w_ref, o_ref):
    o_ref[...] = jnp.dot(h_ref[...], w_ref[...], preferred_element_type=F32)


def _proj(h, w, l):
    n = h.shape[0]
    tm = min(n, 2 * ROW_TILE)
    return pl.pallas_call(
        _proj_kernel,
        grid=(D_PROJ // PROJ_TN, n // tm),
        in_specs=[pl.BlockSpec((tm, D_MODEL), lambda j, i: (i, 0)),
                  pl.BlockSpec((None, D_MODEL, PROJ_TN), lambda j, i: (l, 0, j))],
        out_specs=pl.BlockSpec((tm, PROJ_TN), lambda j, i: (i, j)),
        out_shape=jax.ShapeDtypeStruct((n, D_PROJ), F32),
        compiler_params=_cparams(("parallel", "parallel")),
        name="proj",
    )(h, w)


def _cumsum_rows(x):
    n = x.shape[0]
    row = lax.broadcasted_iota(jnp.int32, x.shape, 0)
    s = 1
    while s < n:
        x = x + jnp.where(row >= s, pltpu.roll(x, s, 0), 0.0)
        s *= 2
    return x


def _group_row(x, group, j):
    n, w = x.shape
    x3 = x.reshape(n // group, group, w)
    return jnp.broadcast_to(x3[:, j:j + 1, :], x3.shape).reshape(n, w)


def _boundary_row(x, c, row):
    if 2 * c >= SUBLANES:
        return _group_row(x, 2 * c, c - 1)
    if c == 2:
        return jnp.where(row % SUBLANES < 4, _group_row(x, SUBLANES, 1), _group_row(x, SUBLANES, 5))
    return jnp.where(row % 2 == 1, pltpu.roll(x, 1, 0), x)


def _hgrn_kernel(qa_ref, fa_ref, ia_ref, ga_ref, lb_ref, nw_ref, s0_ref, prev_ref, o_ref, s_ref, *, L):
    del prev_ref
    G, tt, _ = qa_ref.shape
    nchunk = tt // L
    levels = [c for c in (32, 16, 8, 4, 2, 1) if 2 * c <= L]

    @pl.when(pl.program_id(1) == 0)
    def _():
        s_ref[...] = s0_ref[...]

    lb = lb_ref[...]
    nw = nw_ref[...]
    row = lax.broadcasted_iota(jnp.int32, (L, D_BR), 0)
    row_l = lax.broadcasted_iota(jnp.int32, (L, L), 0)
    col_l = lax.broadcasted_iota(jnp.int32, (L, L), 1)
    heads = [slice(h * D_HEAD, (h + 1) * D_HEAD) for h in range(N_HEAD)]
    level = jnp.where(row_l == col_l, 0.0, -1.0)
    for c in levels:
        pair = ((row_l // (2 * c)) == (col_l // (2 * c))) & (row_l % (2 * c) >= c) & (col_l % (2 * c) < c)
        level = jnp.where(pair, float(c), level)

    def one_sequence(bi, rows):
        f = lb + (1.0 - lb) * _sigmoid(fa_ref[bi, rows, :])
        g = jnp.log(f)
        k = 1.0 - f
        qa = qa_ref[bi, rows, :]
        q = qa * _sigmoid(qa)
        v = ia_ref[bi, rows, :]
        gc = _cumsum_rows(g)

        qg = q * jnp.exp(gc)
        out = [_dot(qg[:, hs], s_ref[bi, h]) for h, hs in enumerate(heads)]

        scores = [jnp.where(level == 0.0, _dot_nt(q[:, hs], k[:, hs]), 0.0) for hs in heads]
        for c in levels:
            diff = gc - _boundary_row(gc, c, row)
            qe = q * jnp.exp(diff)
            ke = k * jnp.exp(-diff)
            for h, hs in enumerate(heads):
                scores[h] = jnp.where(level == float(c), _dot_nt(qe[:, hs], ke[:, hs]), scores[h])

        g_last = gc[L - 1:L, :]
        k_dec = k * jnp.exp(g_last - gc)
        decay = jnp.exp(g_last)
        ga = ga_ref[bi, rows, :]
        gate = ga * _sigmoid(ga)
        for h, hs in enumerate(heads):
            o = out[h] + _dot(scores[h], v[:, hs])
            decay_col = jnp.broadcast_to(decay[:, hs], (D_HEAD, D_HEAD)).T
            s_ref[bi, h] = decay_col * s_ref[bi, h] + _dot_tn(k_dec[:, hs], v[:, hs])
            y = o * lax.rsqrt(jnp.mean(o * o, axis=-1, keepdims=True) + EPS) * nw[:, hs]
            o_ref[bi, rows, hs] = (y * gate[:, hs]).astype(o_ref.dtype)

    def body(ci, carry):
        rows = pl.ds(pl.multiple_of(ci * L, L), L)
        for bi in range(G):
            one_sequence(bi, rows)
        return carry

    lax.fori_loop(0, nchunk, body, 0)


def _hgrn(proj3, lb, nw, s_in, l_in, s_prev, l, depth, L, G, tt):
    b, t, _ = proj3.shape
    col = lambda c: pl.BlockSpec((G, tt, D_BR), lambda i, j: (i, j, c))
    vec = pl.BlockSpec((1, D_BR), lambda i, j: (0, 0))
    st = (G, N_HEAD, D_HEAD, D_HEAD)
    aliases = {}
    if s_prev is None:
        s_prev = jnp.zeros((1,), F32)
    else:
        aliases = {7: 1}
    return pl.pallas_call(
        functools.partial(_hgrn_kernel, L=L),
        grid=(b // G, t // tt),
        in_specs=[col(0), col(1), col(2), col(3), vec, vec, _layer_block(st, l_in, 2),
                  pl.BlockSpec(memory_space=pl.ANY)],
        out_specs=[pl.BlockSpec((G, tt, D_BR), lambda i, j: (i, j, 0)), _layer_block(st, l, 2)],
        out_shape=[jax.ShapeDtypeStruct((b, t, D_BR), BF16),
                   jax.ShapeDtypeStruct((depth, b) + st[1:], F32)],
        input_output_aliases=aliases,
        compiler_params=_cparams(("parallel", "arbitrary")),
        name="hgrn",
    )(proj3, proj3, proj3, proj3, lb, nw, s_in, s_prev)


def _mlstm_kernel(q_ref, k_ref, v_ref, og_ref, sc_ref, sr_ref, bc_ref, br_ref, nw_ref,
                  c0_ref, n0_ref, m0_ref, cp_ref, np_ref, mp_ref, o_ref, c_ref, n_ref, m_ref, *, L):
    del cp_ref, np_ref, mp_ref
    G, tt, _ = q_ref.shape
    nchunk = tt // L

    @pl.when(pl.program_id(1) == 0)
    def _():
        c_ref[...] = c0_ref[...]
        n_ref[...] = n0_ref[...]
        m_ref[...] = m0_ref[...]

    nw = nw_ref[...]
    bias_c = bc_ref[...]
    bias_r = br_ref[...]
    row_l = lax.broadcasted_iota(jnp.int32, (L, L), 0)
    col_l = lax.broadcasted_iota(jnp.int32, (L, L), 1)
    causal = row_l >= col_l
    tri = jnp.where(causal, 1.0, 0.0).astype(BF16)
    tri_t = jnp.where(row_l <= col_l, 1.0, 0.0).astype(BF16)
    row_c = lax.broadcasted_iota(jnp.int32, (L, N_SMALL), 0)
    ones = jnp.ones((L, D_HEAD), F32)
    scale = D_HEAD ** -0.5
    heads = [slice(h * D_HEAD, (h + 1) * D_HEAD) for h in range(N_HEAD)]
    group = G if L < CHUNK else min(G, 2)

    def gates(bi, ci, rows):
        pre_c = sc_ref[bi, rows, :] + bias_c
        pre_r = sr_ref[bi, ci] + bias_r
        hi, mid, lo = _split3(_log_sigmoid(pre_c))
        b_c = (jnp.dot(tri, hi, preferred_element_type=F32) + jnp.dot(tri, mid, preferred_element_type=F32)
               + jnp.dot(tri, lo, preferred_element_type=F32))
        hi, mid, lo = _split3(_log_sigmoid(pre_r))
        b_r = (jnp.dot(hi, tri_t, preferred_element_type=F32) + jnp.dot(mid, tri_t, preferred_element_type=F32)
               + jnp.dot(lo, tri_t, preferred_element_type=F32))
        b = pltpu.roll(b_c, N_SMALL - N_HEAD, 1)
        a = pre_c - b
        cm = a
        s = 1
        while s < L:
            cm = jnp.maximum(cm, jnp.where(row_c >= s, pltpu.roll(cm, s, 0), -jnp.inf))
            s *= 2
        m_old = m_ref[bi]
        big_m = jnp.maximum(m_old, cm)
        m_last = big_m[L - 1:L, :]
        m_ref[bi] = b[L - 1:L, :] + m_last
        return dict(a_r=pre_r[0:N_HEAD, :] - b_r[N_HEAD:2 * N_HEAD, :], big_m=big_m, m_old=m_old,
                    w_inter=jnp.exp(m_old - big_m), n_exp=jnp.exp(-(b + big_m)),
                    w_state=jnp.exp(a - m_last), carry=jnp.exp(m_old - m_last))

    def body(ci, carry):
        rows = pl.ds(pl.multiple_of(ci * L, L), L)
        for g0 in range(0, G, group):
            seqs = range(g0, g0 + group)
            sh = {bi: gates(bi, ci, rows) for bi in seqs}
            probs = [(bi, h) for bi in seqs for h in range(N_HEAD)]
            qk, wi_b, intra, qc, qn = {}, {}, {}, {}, {}
            for p in probs:
                bi, h = p
                g = sh[bi]
                m_b = jnp.broadcast_to(g["big_m"][:, h:h + 1], (L, D_HEAD))
                wi_b[p] = jnp.exp(g["m_old"][:, h:h + 1] - m_b)
                w = jnp.where(causal, jnp.exp(g["a_r"][h:h + 1, :] - m_b[:, :L]), 0.0)
                q = q_ref[bi, rows, heads[h]] * scale
                qk[p] = _dot_nt(q, k_ref[bi, rows, heads[h]]) * w
            for p in probs:
                bi, h = p
                q = q_ref[bi, rows, heads[h]] * scale
                intra[p] = _dot(qk[p], jnp.concatenate([v_ref[bi, rows, heads[h]], ones], axis=1))
                qc[p] = _dot(q, c_ref[bi, h])
                qn[p] = _dot_nt(q, jnp.broadcast_to(n_ref[bi, h], (D_HEAD, D_HEAD)))
            for p in probs:
                bi, h = p
                g = sh[bi]
                num = intra[p][:, :D_HEAD] + wi_b[p] * qc[p]
                den = intra[p][:, D_HEAD:D_HEAD + 1] + g["w_inter"][:, h:h + 1] * qn[p][:, 0:1]
                r = 1.0 / jnp.maximum(jnp.abs(den), g["n_exp"][:, h:h + 1])
                t = r * lax.rsqrt(r * r * jnp.mean(num * num, axis=-1, keepdims=True) + EPS)
                gate = _sigmoid(og_ref[bi, rows, heads[h]])
                o_ref[bi, rows, heads[h]] = (num * t * nw[:, heads[h]] * gate).astype(o_ref.dtype)
            for p in probs:
                bi, h = p
                g = sh[bi]
                k = k_ref[bi, rows, heads[h]]
                kw = g["w_state"][:, h:h + 1] * k
                cd = g["carry"][:, h:h + 1]
                c_ref[bi, h] = cd * c_ref[bi, h] + _dot_tn(kw, v_ref[bi, rows, heads[h]])
                n_ref[bi, h] = cd * n_ref[bi, h] + jnp.sum(kw, axis=0, keepdims=True)
        return carry

    lax.fori_loop(0, nchunk, body, 0)


def _mlstm(proj3, small_row, bias_c, bias_r, nw, st_in, l_in, st_prev, l, depth, L, G, tt):
    b, t, _ = proj3.shape
    col = lambda c: pl.BlockSpec((G, tt, D_BR), lambda i, j: (i, j, c))
    full2 = lambda shape: pl.BlockSpec(shape, lambda i, j: (0, 0))
    shapes = [(G, N_HEAD, D_HEAD, D_HEAD), (G, N_HEAD, 1, D_HEAD), (G, 1, N_SMALL)]
    aliases = {}
    if st_prev is None:
        st_prev = [jnp.zeros((1,), F32)] * 3
    else:
        aliases = {12: 1, 13: 2, 14: 3}
    any_spec = pl.BlockSpec(memory_space=pl.ANY)
    return pl.pallas_call(
        functools.partial(_mlstm_kernel, L=L),
        grid=(b // G, t // tt),
        in_specs=[col(4), col(5), col(6), col(7),
                  pl.BlockSpec((G, tt, N_SMALL), lambda i, j: (i, j, COL_SMALL // N_SMALL)),
                  pl.BlockSpec((G, tt // L, 2 * N_HEAD, L), lambda i, j: (i, j, 0, 0)),
                  full2((1, N_SMALL)), full2((2 * N_HEAD, 1)), full2((1, D_BR))]
                 + [_layer_block(s, l_in, 2) for s in shapes] + [any_spec] * 3,
        out_specs=[pl.BlockSpec((G, tt, D_BR), lambda i, j: (i, j, 0))] + [_layer_block(s, l, 2) for s in shapes],
        out_shape=[jax.ShapeDtypeStruct((b, t, D_BR), BF16)]
                  + [jax.ShapeDtypeStruct((depth, b) + s[1:], F32) for s in shapes],
        input_output_aliases=aliases,
        compiler_params=_cparams(("parallel", "arbitrary")),
        name="mlstm",
    )(proj3, proj3, proj3, proj3, proj3, small_row, bias_c, bias_r, nw, *st_in, *st_prev)


PAD_C = 8
PAD_D = 32
CONV_ROWS = 32


def _conv_kernel(bc_ref, cc_ref, xc_ref, ad_ref, gd_ref, bufc_ref, bufd_ref, wc_ref, wd_ref, bd_ref,
                 lg_ref, lbias_ref, pc_ref, pd_ref, oc_ref, od_ref, nbc_ref, nbd_ref, extc, extd, shc, shd):
    del pc_ref, pd_ref
    bblk, tt, _ = bc_ref.shape
    hc, hd = CONV_C - 1, CONV_D - 1
    rt = min(tt, CONV_ROWS)
    offs_c = [PAD_C - hc + j for j in range(CONV_C)]
    offs_d = [PAD_D - hd + j for j in range(CONV_D)]
    unaligned_c = [o for o in offs_c if o % SUBLANES]
    span_d = tt + PAD_D - SUBLANES

    @pl.when(pl.program_id(1) == 0)
    def _():
        extc[:, PAD_C - hc:PAD_C, :] = bufc_ref[...]
        extd[:, PAD_D - hd:PAD_D, :] = bufd_ref[...]

    extc[:, PAD_C:PAD_C + tt, :] = cc_ref[...] * xc_ref[...]
    extd[:, PAD_D:PAD_D + tt, :] = ad_ref[...] * _sigmoid(gd_ref[...])

    def tile_rows(w8):
        return w8 if rt == SUBLANES else jnp.concatenate([w8] * (rt // SUBLANES), axis=0)

    def seq_body(bi, carry):
        for res in range(1, SUBLANES):
            shd[res, 0:span_d, :] = extd[bi, res:res + span_d, :]
        for i, o in enumerate(unaligned_c):
            shc[i, 0:tt, :] = extc[bi, o:o + tt, :]

        def tile_body(ti, c2):
            r0 = pl.multiple_of(ti * rt, rt)
            acc = jnp.zeros((rt, D_BR), F32)
            for j, o in enumerate(offs_c):
                if o % SUBLANES:
                    x = shc[unaligned_c.index(o), pl.ds(r0, rt), :]
                else:
                    x = extc[bi, pl.ds(pl.multiple_of(r0 + o, SUBLANES), rt), :]
                acc = acc + tile_rows(wc_ref[j]) * x
            oc_ref[bi, pl.ds(r0, rt), :] = (bc_ref[bi, pl.ds(r0, rt), :] * acc).astype(oc_ref.dtype)
            acc = jnp.zeros((rt, D_BR), F32)
            for j, o in enumerate(offs_d):
                res = o % SUBLANES
                rows = pl.ds(pl.multiple_of(r0 + (o - res), SUBLANES), rt)
                x = shd[res, rows, :] if res else extd[bi, rows, :]
                acc = acc + tile_rows(wd_ref[j]) * x
            y = acc + bd_ref[...]
            yc = y - jnp.mean(y, axis=-1, keepdims=True)
            z = yc * lax.rsqrt(jnp.mean(yc * yc, axis=-1, keepdims=True) + EPS) * lg_ref[...] + lbias_ref[...]
            od_ref[bi, pl.ds(r0, rt), :] = (z * _sigmoid(z)).astype(od_ref.dtype)
            return c2

        lax.fori_loop(0, tt // rt, tile_body, 0)
        return carry

    lax.fori_loop(0, bblk, seq_body, 0)

    new_c = extc[:, PAD_C + tt - hc:PAD_C + tt, :]
    new_d = extd[:, PAD_D + tt - hd:PAD_D + tt, :]
    nbc_ref[...] = new_c
    nbd_ref[...] = new_d
    extc[:, PAD_C - hc:PAD_C, :] = new_c
    extd[:, PAD_D - hd:PAD_D, :] = new_d


def _conv(proj3, buf_in, l_in, buf_prev, l, depth, wc, wd, bd, lg, lbias, bblk, tt):
    b, t, _ = proj3.shape
    c0 = COL_CONV // D_BR
    col = lambda c: pl.BlockSpec((bblk, tt, D_BR), lambda i, j: (i, j, c0 + c))
    full2 = lambda shape: pl.BlockSpec(shape, lambda i, j: (0, 0))
    full3 = lambda shape: pl.BlockSpec(shape, lambda i, j: (0, 0, 0))
    rows8 = lambda w: jnp.broadcast_to(w[:, None, :], (w.shape[0], SUBLANES, D_BR))
    shapes = [(bblk, CONV_C - 1, D_BR), (bblk, CONV_D - 1, D_BR)]
    outc =pl.BlockSpec((bblk, tt, D_BR), lambda i, j: (i, j, 0))
    aliases = {}
    if buf_prev is None:
        buf_prev = [jnp.zeros((1,), F32)] * 2
    else:
        aliases = {12: 2, 13: 3}
    any_spec = pl.BlockSpec(memory_space=pl.ANY)
    return pl.pallas_call(
        _conv_kernel,
        grid=(b // bblk, t // tt),
        in_specs=[col(0), col(1), col(2), col(3), col(4)] + [_layer_block(s, l_in, 2) for s in shapes]
                 + [full3((CONV_C, SUBLANES, D_BR)), full3((CONV_D, SUBLANES, D_BR)), full2((1, D_BR)),
                    full2((1, D_BR)), full2((1, D_BR)), any_spec, any_spec],
        out_specs=[outc, outc] + [_layer_block(s, l, 2) for s in shapes],
        out_shape=[jax.ShapeDtypeStruct((b, t, D_BR), BF16), jax.ShapeDtypeStruct((b, t, D_BR), BF16)]
                  + [jax.ShapeDtypeStruct((depth, b) + s[1:], F32) for s in shapes],
        scratch_shapes=[pltpu.VMEM((bblk, PAD_C + tt, D_BR), F32), pltpu.VMEM((bblk, PAD_D + tt, D_BR), F32),
                        pltpu.VMEM((CONV_C - 1, tt, D_BR), F32), pltpu.VMEM((SUBLANES, PAD_D + tt, D_BR), F32)],
        input_output_aliases=aliases,
        compiler_params=_cparams(("parallel", "arbitrary")),
        name="conv",
    )(proj3, proj3, proj3, proj3, proj3, *buf_in, rows8(wc), rows8(wd), bd, lg, lbias, *buf_prev)


FF_TILE = 1024
MIX_VMEM_LIMIT = 60 * 1024 * 1024


def _mix_kernel(x_ref, oa_ref, ob_ref, oc_ref, od_ref, gt_ref, wb_ref, wo_ref, nm_ref, wu_ref, wdn_ref,
                nn_ref, *out_refs, last):
    merged = None
    for n, br in enumerate((oa_ref, ob_ref, oc_ref, od_ref)):
        gate = _sigmoid(gt_ref[:, n * D_MODEL:(n + 1) * D_MODEL])
        term = gate * jnp.dot(br[...], wb_ref[n], preferred_element_type=F32)
        merged = term if merged is None else merged + term
    x = x_ref[...] + jnp.dot(merged.astype(BF16), wo_ref[...], preferred_element_type=F32)
    hm = _rms(x, nm_ref[...]).astype(BF16)
    for c in range(D_FF // FF_TILE):
        cs = slice(c * FF_TILE, (c + 1) * FF_TILE)
        up = jnp.maximum(jnp.dot(hm, wu_ref[:, cs], preferred_element_type=F32), 0.0)
        x = x + jnp.dot((up * up).astype(BF16), wdn_ref[cs, :], preferred_element_type=F32)
    if last:
        out_refs[0][...] = _rms(x, nn_ref[...])
    else:
        out_refs[0][...] = x
        out_refs[1][...] = _rms(x, nn_ref[...]).astype(BF16)


def _mix(x, oa, ob, oc, od, proj, wb, wo, nm, wu, wdn, nn, l, last):
    n = x.shape[0]
    tm = min(n, ROW_TILE)
    row = lambda w: pl.BlockSpec((tm, w), lambda i: (i, 0))
    once = pl.Buffered(1)
    full2 = lambda shape: pl.BlockSpec(shape, lambda i: (0, 0))
    layer3 = lambda shape: pl.BlockSpec((None,) + shape, lambda i: (l, 0, 0), pipeline_mode=once)
    out_specs = [row(D_MODEL)] if last else [row(D_MODEL), row(D_MODEL)]
    out_shape = [jax.ShapeDtypeStruct((n, D_MODEL), F32)]
    if not last:
        out_shape.append(jax.ShapeDtypeStruct((n, D_MODEL), BF16))
    return pl.pallas_call(
        functools.partial(_mix_kernel, last=last),
        grid=(n // tm,),
        in_specs=[row(D_MODEL), row(D_BR), row(D_BR), row(D_BR), row(D_BR),
                  pl.BlockSpec((tm, N_GATE), lambda i: (i, COL_GATE // N_GATE)),
                  pl.BlockSpec((None, N_BRANCH, D_BR, D_MODEL), lambda i: (l, 0, 0, 0), pipeline_mode=once),
                  layer3((D_MODEL, D_MODEL)), full2((1, D_MODEL)), layer3((D_MODEL, D_FF)),
                  layer3((D_FF, D_MODEL)), full2((1, D_MODEL))],
        out_specs=out_specs,
        out_shape=out_shape,
        compiler_params=_cparams(("parallel",), MIX_VMEM_LIMIT),
        name="mix",
    )(x, oa, ob, oc, od, proj, wb, wo, nm, wu, wdn, nn)


def _layer(x, h, b, t, st_in, l_in, st_prev, l, depth, lw, wts, last):
    (lb, n_hgrn, bias_c, bias_r, n_mlstm, wc, wd, bd, lg, lbias, n_mlp, n_next) = lw
    w_in, wb, wo, wu, wdn = wts
    L = math.gcd(t, CHUNK)
    G = min(b, SEQ_GROUP)
    proj = _proj(h, w_in, l)
    proj3 = proj.reshape(b, t, D_PROJ)
    small = proj3[:, :, COL_SMALL:COL_SMALL + 2 * N_HEAD]
    small_row = jnp.swapaxes(small.reshape(b, t // L, L, 2 * N_HEAD), 2, 3)

    prev = (None,) * 6 if st_prev is None else st_prev
    g_h, tt_h = (2, 512) if t >= 512 else (G, t)
    oa, s_hgrn = _hgrn(proj3, lb, n_hgrn, st_in[0], l_in, prev[0], l, depth, L, g_h, tt_h)
    tt_m = min(t, 128)
    ob, s_c, s_n, s_m = _mlstm(proj3, small_row, bias_c, bias_r, n_mlstm, st_in[1:4], l_in,
                               None if st_prev is None else prev[1:4], l, depth, L, G, tt_m)
    g_c, tt_c = (1, 512) if t >= 512 else (G, t)
    oc, od, buf_c, buf_d = _conv(proj3, st_in[4:6], l_in, None if st_prev is None else prev[4:6], l, depth,
                                 wc, wd, bd, lg, lbias, g_c, tt_c)
    flat = lambda a: a.reshape(b * t, D_BR)
    outs = _mix(x, flat(oa), flat(ob), flat(oc), flat(od), proj, wb, wo, n_mlp, wu, wdn, n_next, l, last)
    return outs, (s_hgrn, s_c, s_n, s_m, buf_c, buf_d)


def kernel(x_prompt, x_sample, state_hgrn, state_mlstm_c, state_mlstm_n, state_mlstm_m, state_conv_short, state_conv_conformer, w_in, w_branch, w_out, lb_logits, norm_hgrn, mlstm_i_bias, mlstm_f_bias, norm_mlstm, conv_short_w, conv_conformer_w, conv_conformer_b, ln_conformer_g, ln_conformer_b, norm_mix, norm_mlp, w_up, w_down, norm_final):
    depth = w_in.shape[0]
    bp, tp, _ = x_prompt.shape
    bs, ts, _ = x_sample.shape
    lower_bounds = _lower_bounds(lb_logits)

    o_small = N_MAIN
    o_conv = N_MAIN + 2 * N_HEAD
    o_gate = o_conv + N_CONV
    w_in_p = jnp.concatenate(
        [w_in[:, :, :N_MAIN], w_in[:, :, o_gate:o_gate + N_GATE], w_in[:, :, o_conv:o_conv + N_CONV],
         w_in[:, :, o_small:o_small + 2 * N_HEAD],
         jnp.zeros((depth, D_MODEL, N_SMALL - 2 * N_HEAD), w_in.dtype)], axis=2).astype(BF16)
    wts = (w_in_p,) + tuple(a.astype(BF16) for a in (w_branch, w_out, w_up, w_down))
    gate_bias = jnp.concatenate([mlstm_i_bias, mlstm_f_bias], axis=1).astype(F32)
    bias_c = jnp.pad(gate_bias, ((0, 0), (0, N_SMALL - 2 * N_HEAD)))[:, None, :]
    bias_r = gate_bias[:, :, None]
    row = lambda a, l: a[l][None, :].astype(F32)

    zero_st = (jnp.zeros((1, bp, N_HEAD, D_HEAD, D_HEAD), F32), jnp.zeros((1, bp, N_HEAD, D_HEAD, D_HEAD), F32),
               jnp.zeros((1, bp, N_HEAD, 1, D_HEAD), F32), jnp.zeros((1, bp, 1, N_SMALL), F32),
               jnp.zeros((1, bp, CONV_C - 1, D_BR), F32), jnp.zeros((1, bp, CONV_D - 1, D_BR), F32))
    m_lanes = jnp.pad(state_mlstm_m.astype(F32), ((0, 0), (0, 0), (0, N_SMALL - N_HEAD)))[:, :, None, :]
    past_st = (state_hgrn, state_mlstm_c, state_mlstm_n[:, :, :, None, :], m_lanes,
               state_conv_short, state_conv_conformer)

    xp = x_prompt.reshape(bp * tp, D_MODEL)
    xs = x_sample.reshape(bs * ts, D_MODEL)
    hp = _norm(xp, row(norm_mix, 0))
    hs = _norm(xs, row(norm_mix, 0))
    st_p = st_s = None
    for l in range(depth):
        last = l == depth - 1
        n_next = norm_final[None, :].astype(F32) if last else row(norm_mix, l + 1)
        lw = (lower_bounds[l][None, :], row(norm_hgrn, l), bias_c[l], bias_r[l], row(norm_mlstm, l),
              conv_short_w[l], conv_conformer_w[l], row(conv_conformer_b, l), row(ln_conformer_g, l),
              row(ln_conformer_b, l), row(norm_mlp, l), n_next)
        outs_p, st_p = _layer(xp, hp, bp, tp, zero_st, 0, st_p, l, depth, lw, wts, last)
        outs_s, st_s = _layer(xs, hs, bs, ts, past_st, l, st_s, l, depth, lw, wts, last)
        if last:
            xp, xs = outs_p[0], outs_s[0]
        else:
            (xp, hp), (xs, hs) = outs_p, outs_s

    def finish(x, b, t, st):
        s_hgrn, s_c, s_n, s_m, buf_c, buf_d = st
        return (x.reshape(b, t, D_MODEL), s_hgrn, s_c, s_n[:, :, :, 0, :], s_m[:, :, 0, :N_HEAD], buf_c, buf_d)

    fp = finish(xp, bp, tp, st_p)
    fs = finish(xs, bs, ts, st_s)
    outs = []
    for a, c in zip(fp, fs):
        outs += [a, c]
    return tuple(outs)
```

```python
import functools
import math

import jax
import jax.numpy as jnp
from jax import lax
from jax.experimental import pallas as pl
from jax.experimental.pallas import tpu as pltpu

F32 = jnp.float32
BF16 = jnp.bfloat16

D_MODEL = 1024
N_HEAD = 4
D_HEAD = 128
D_BR = 512
N_BRANCH = 4
CONV_C = 3
CONV_D = 31
D_FF = 4 * D_MODEL
CHUNK = 64
EPS = 1e-6
SUBLANES = 8
LANES = 128

N_MAIN = 8 * D_BR
N_GATE = N_BRANCH * D_MODEL
N_CONV = 5 * D_BR
N_SMALL = LANES
D_PROJ = N_MAIN + N_GATE + N_CONV + N_SMALL
COL_GATE = N_MAIN
COL_CONV = N_MAIN + N_GATE
COL_SMALL = N_MAIN + N_GATE + N_CONV
PROJ_TN = D_PROJ // 5

VMEM_LIMIT = 56 * 1024 * 1024
ROW_TILE = 512
SEQ_GROUP = 8


def _cparams(sem, vmem_limit=VMEM_LIMIT):
    return pltpu.CompilerParams(dimension_semantics=sem, vmem_limit_bytes=vmem_limit)


def _dot(a, b):
    return jnp.dot(a.astype(BF16), b.astype(BF16), preferred_element_type=F32)


def _dot_nt(a, b):
    return lax.dot_general(a.astype(BF16), b.astype(BF16), (((1,), (1,)), ((), ())),
                           preferred_element_type=F32)


def _dot_tn(a, b):
    return lax.dot_general(a.astype(BF16), b.astype(BF16), (((0,), (0,)), ((), ())),
                           preferred_element_type=F32)


def _split3(x):
    hi = x.astype(BF16)
    r = x - hi.astype(F32)
    mid = r.astype(BF16)
    lo = (r - mid.astype(F32)).astype(BF16)
    return hi, mid, lo


def _sigmoid(x):
    return jax.nn.sigmoid(x)


def _log_sigmoid(x):
    return jnp.minimum(x, 0.0) - jnp.log1p(jnp.exp(-jnp.abs(x)))


def _rms(x, g):
    return x * lax.rsqrt(jnp.mean(x * x, axis=-1, keepdims=True) + EPS) * g


def _layer_block(shape, l, grid_rank):
    zeros = (0,) * (len(shape) - 1)
    if grid_rank == 2:
        return pl.BlockSpec((None,) + shape, lambda i, j: (l, i) + zeros)
    return pl.BlockSpec((None,) + shape, lambda i: (l, i) + zeros)


def _lb_kernel(x_ref, o_ref):
    x = x_ref[...]
    depth = x.shape[0]
    e = jnp.exp(x - jnp.max(x, axis=0, keepdims=True))
    s = e / jnp.sum(e, axis=0, keepdims=True)
    acc = jnp.zeros_like(s[0:1])
    for i in range(depth):
        acc = acc + s[i:i + 1]
        o_ref[i:i + 1, :] = acc - s[0:1]


def _lower_bounds(lb_logits):
    return pl.pallas_call(
        _lb_kernel, out_shape=jax.ShapeDtypeStruct(lb_logits.shape, F32), name="lower_bounds",
    )(lb_logits.astype(F32))


def _norm_kernel(x_ref, g_ref, o_ref):
    o_ref[...] = _rms(x_ref[...], g_ref[...]).astype(o_ref.dtype)


def _norm(x, g):
    n = x.shape[0]
    tm = min(n, ROW_TILE)
    return pl.pallas_call(
        _norm_kernel,
        grid=(n // tm,),
        in_specs=[pl.BlockSpec((tm, D_MODEL), lambda i: (i, 0)), pl.BlockSpec((1, D_MODEL), lambda i: (0, 0))],
        out_specs=pl.BlockSpec((tm, D_MODEL), lambda i: (i, 0)),
        out_shape=jax.ShapeDtypeStruct((n, D_MODEL), BF16),
        compiler_params=_cparams(("parallel",)),
        name="norm",
    )(x, g)


def _proj_kernel(h_ref, w_ref, o_ref):
    o_ref[...] = jnp.dot(h_ref[...], w_ref[...], preferred_element_type=F32)


def _proj(h, w, l):
    n = h.shape[0]
    tm = min(n, 2 * ROW_TILE)
    return pl.pallas_call(
        _proj_kernel,
        grid=(D_PROJ // PROJ_TN, n // tm),
        in_specs=[pl.BlockSpec((tm, D_MODEL), lambda j, i: (i, 0)),
                  pl.BlockSpec((None, D_MODEL, PROJ_TN), lambda j, i: (l, 0, j))],
        out_specs=pl.BlockSpec((tm, PROJ_TN), lambda j, i: (i, j)),
        out_shape=jax.ShapeDtypeStruct((n, D_PROJ), F32),
        compiler_params=_cparams(("parallel", "parallel")),
        name="proj",
    )(h, w)


def _cumsum_rows(x):
    n = x.shape[0]
    row = lax.broadcasted_iota(jnp.int32, x.shape, 0)
    s = 1
    while s < n:
        x = x + jnp.where(row >= s, pltpu.roll(x, s, 0), 0.0)
        s *= 2
    return x


def _group_row(x, group, j):
    n, w = x.shape
    x3 = x.reshape(n // group, group, w)
    return jnp.broadcast_to(x3[:, j:j + 1, :], x3.shape).reshape(n, w)


def _boundary_row(x, c, row):
    if 2 * c >= SUBLANES:
        return _group_row(x, 2 * c, c - 1)
    if c == 2:
        return jnp.where(row % SUBLANES < 4, _group_row(x, SUBLANES, 1), _group_row(x, SUBLANES, 5))
    return jnp.where(row % 2 == 1, pltpu.roll(x, 1, 0), x)


def _hgrn_kernel(qa_ref, fa_ref, ia_ref, ga_ref, lb_ref, nw_ref, s0_ref, prev_ref, o_ref, s_ref, *, L):
    del prev_ref
    G, tt, _ = qa_ref.shape
    nchunk = tt // L
    levels = [c for c in (32, 16, 8, 4, 2, 1) if 2 * c <= L]

    @pl.when(pl.program_id(1) == 0)
    def _():
        s_ref[...] = s0_ref[...]

    lb = lb_ref[...]
    nw = nw_ref[...]
    row = lax.broadcasted_iota(jnp.int32, (L, D_BR), 0)
    row_l = lax.broadcasted_iota(jnp.int32, (L, L), 0)
    col_l = lax.broadcasted_iota(jnp.int32, (L, L), 1)
    heads = [slice(h * D_HEAD, (h + 1) * D_HEAD) for h in range(N_HEAD)]
    level = jnp.where(row_l == col_l, 0.0, -1.0)
    for c in levels:
        pair = ((row_l // (2 * c)) == (col_l // (2 * c))) & (row_l % (2 * c) >= c) & (col_l % (2 * c) < c)
        level = jnp.where(pair, float(c), level)

    def one_sequence(bi, rows):
        f = lb + (1.0 - lb) * _sigmoid(fa_ref[bi, rows, :])
        g = jnp.log(f)
        k = 1.0 - f
        qa = qa_ref[bi, rows, :]
        q = qa * _sigmoid(qa)
        v = ia_ref[bi, rows, :]
        gc = _cumsum_rows(g)

        qg = q * jnp.exp(gc)
        out = [_dot(qg[:, hs], s_ref[bi, h]) for h, hs in enumerate(heads)]

        scores = [jnp.where(level == 0.0, _dot_nt(q[:, hs], k[:, hs]), 0.0) for hs in heads]
        for c in levels:
            diff = gc - _boundary_row(gc, c, row)
            qe = q * jnp.exp(diff)
            ke = k * jnp.exp(-diff)
            for h, hs in enumerate(heads):
                scores[h] = jnp.where(level == float(c), _dot_nt(qe[:, hs], ke[:, hs]), scores[h])

        g_last = gc[L - 1:L, :]
        k_dec = k * jnp.exp(g_last - gc)
        decay = jnp.exp(g_last)
        ga = ga_ref[bi, rows, :]
        gate = ga * _sigmoid(ga)
        for h, hs in enumerate(heads):
            o = out[h] + _dot(scores[h], v[:, hs])
            decay_col = jnp.broadcast_to(decay[:, hs], (D_HEAD, D_HEAD)).T
            s_ref[bi, h] = decay_col * s_ref[bi, h] + _dot_tn(k_dec[:, hs], v[:, hs])
            y = o * lax.rsqrt(jnp.mean(o * o, axis=-1, keepdims=True) + EPS) * nw[:, hs]
            o_ref[bi, rows, hs] = (y * gate[:, hs]).astype(o_ref.dtype)

    def body(ci, carry):
        rows = pl.ds(pl.multiple_of(ci * L, L), L)
        for bi in range(G):
            one_sequence(bi, rows)
        return carry

    lax.fori_loop(0, nchunk, body, 0)


def _hgrn(proj3, lb, nw, s_in, l_in, s_prev, l, depth, L, G, tt):
    b, t, _ = proj3.shape
    col = lambda c: pl.BlockSpec((G, tt, D_BR), lambda i, j: (i, j, c))
    vec = pl.BlockSpec((1, D_BR), lambda i, j: (0, 0))
    st = (G, N_HEAD, D_HEAD, D_HEAD)
    aliases = {}
    if s_prev is None:
        s_prev = jnp.zeros((1,), F32)
    else:
        aliases = {7: 1}
    return pl.pallas_call(
        functools.partial(_hgrn_kernel, L=L),
        grid=(b // G, t // tt),
        in_specs=[col(0), col(1), col(2), col(3), vec, vec, _layer_block(st, l_in, 2),
                  pl.BlockSpec(memory_space=pl.ANY)],
        out_specs=[pl.BlockSpec((G, tt, D_BR), lambda i, j: (i, j, 0)), _layer_block(st, l, 2)],
        out_shape=[jax.ShapeDtypeStruct((b, t, D_BR), BF16),
                   jax.ShapeDtypeStruct((depth, b) + st[1:], F32)],
        input_output_aliases=aliases,
        compiler_params=_cparams(("parallel", "arbitrary")),
        name="hgrn",
    )(proj3, proj3, proj3, proj3, lb, nw, s_in, s_prev)


def _mlstm_kernel(q_ref, k_ref, v_ref, og_ref, sc_ref, sr_ref, bc_ref, br_ref, nw_ref,
                  c0_ref, n0_ref, m0_ref, cp_ref, np_ref, mp_ref, o_ref, c_ref, n_ref, m_ref, *, L):
    del cp_ref, np_ref, mp_ref
    G, tt, _ = q_ref.shape
    nchunk = tt // L

    @pl.when(pl.program_id(1) == 0)
    def _():
        c_ref[...] = c0_ref[...]
        n_ref[...] = n0_ref[...]
        m_ref[...] = m0_ref[...]

    nw = nw_ref[...]
    bias_c = bc_ref[...]
    bias_r = br_ref[...]
    row_l = lax.broadcasted_iota(jnp.int32, (L, L), 0)
    col_l = lax.broadcasted_iota(jnp.int32, (L, L), 1)
    causal = row_l >= col_l
    tri = jnp.where(causal, 1.0, 0.0).astype(BF16)
    tri_t = jnp.where(row_l <= col_l, 1.0, 0.0).astype(BF16)
    row_c = lax.broadcasted_iota(jnp.int32, (L, N_SMALL), 0)
    ones = jnp.ones((L, D_HEAD), F32)
    scale = D_HEAD ** -0.5
    heads = [slice(h * D_HEAD, (h + 1) * D_HEAD) for h in range(N_HEAD)]
    group = G

    def gates(bi, ci, rows):
        pre_c = sc_ref[bi, rows, :] + bias_c
        pre_r = sr_ref[bi, ci] + bias_r
        hi, mid, lo = _split3(_log_sigmoid(pre_c))
        b_c = (jnp.dot(tri, hi, preferred_element_type=F32) + jnp.dot(tri, mid, preferred_element_type=F32)
               + jnp.dot(tri, lo, preferred_element_type=F32))
        hi, mid, lo = _split3(_log_sigmoid(pre_r))
        b_r = (jnp.dot(hi, tri_t, preferred_element_type=F32) + jnp.dot(mid, tri_t, preferred_element_type=F32)
               + jnp.dot(lo, tri_t, preferred_element_type=F32))
        b = pltpu.roll(b_c, N_SMALL - N_HEAD, 1)
        a = pre_c - b
        cm = a
        s = 1
        while s < L:
            cm = jnp.maximum(cm, jnp.where(row_c >= s, pltpu.roll(cm, s, 0), -jnp.inf))
            s *= 2
        m_old = m_ref[bi]
        big_m = jnp.maximum(m_old, cm)
        m_last = big_m[L - 1:L, :]
        m_ref[bi] = b[L - 1:L, :] + m_last
        return dict(a_r=pre_r[0:N_HEAD, :] - b_r[N_HEAD:2 * N_HEAD, :], big_m=big_m, m_old=m_old,
                    w_inter=jnp.exp(m_old - big_m), n_exp=jnp.exp(-(b + big_m)),
                    w_state=jnp.exp(a - m_last), carry=jnp.exp(m_old - m_last))

    def body(ci, carry):
        rows = pl.ds(pl.multiple_of(ci * L, L), L)
        for g0 in range(0, G, group):
            seqs = range(g0, g0 + group)
            sh = {bi: gates(bi, ci, rows) for bi in seqs}
            probs = [(bi, h) for bi in seqs for h in range(N_HEAD)]
            qk, wi_b, intra, qc, qn = {}, {}, {}, {}, {}
            for p in probs:
                bi, h = p
                g = sh[bi]
                m_b = jnp.broadcast_to(g["big_m"][:, h:h + 1], (L, D_HEAD))
                wi_b[p] = jnp.exp(g["m_old"][:, h:h + 1] - m_b)
                w = jnp.where(causal, jnp.exp(g["a_r"][h:h + 1, :] - m_b[:, :L]), 0.0)
                q = q_ref[bi, rows, heads[h]] * scale
                qk[p] = _dot_nt(q, k_ref[bi, rows, heads[h]]) * w
            for p in probs:
                bi, h = p
                q = q_ref[bi, rows, heads[h]] * scale
                intra[p] = _dot(qk[p], jnp.concatenate([v_ref[bi, rows, heads[h]], ones], axis=1))
                qc[p] = _dot(q, c_ref[bi, h])
                qn[p] = _dot_nt(q, jnp.broadcast_to(n_ref[bi, h], (D_HEAD, D_HEAD)))
            for p in probs:
                bi, h = p
                g = sh[bi]
                num = intra[p][:, :D_HEAD] + wi_b[p] * qc[p]
                den = intra[p][:, D_HEAD + h:D_HEAD + h + 1] + g["w_inter"][:, h:h + 1] * qn[p][:, h:h + 1]
                r = 1.0 / jnp.maximum(jnp.abs(den), g["n_exp"][:, h:h + 1])
                t = r * lax.rsqrt(r * r * jnp.mean(num * num, axis=-1, keepdims=True) + EPS)
                gate = _sigmoid(og_ref[bi, rows, heads[h]])
                o_ref[bi, rows, heads[h]] = (num * t * nw[:, heads[h]] * gate).astype(o_ref.dtype)
            for p in probs:
                bi, h = p
                g = sh[bi]
                k = k_ref[bi, rows, heads[h]]
                kw = g["w_state"][:, h:h + 1] * k
                cd = g["carry"][:, h:h + 1]
                c_ref[bi, h] = cd * c_ref[bi, h] + _dot_tn(kw, v_ref[bi, rows, heads[h]])
                n_ref[bi, h] = cd * n_ref[bi, h] + jnp.sum(kw, axis=0, keepdims=True)
        return carry

    lax.fori_loop(0, nchunk, body, 0)


def _mlstm(proj3, small_row, bias_c, bias_r, nw, st_in, l_in, st_prev, l, depth, L, G, tt):
    b, t, _ = proj3.shape
    col = lambda c: pl.BlockSpec((G, tt, D_BR), lambda i, j: (i, j, c))
    full2 = lambda shape: pl.BlockSpec(shape, lambda i, j: (0, 0))
    shapes = [(G, N_HEAD, D_HEAD, D_HEAD), (G, N_HEAD, 1, D_HEAD), (G, 1, N_SMALL)]
    aliases = {}
    if st_prev is None:
        st_prev = [jnp.zeros((1,), F32)] * 3
    else:
        aliases = {12: 1, 13: 2, 14: 3}
    any_spec = pl.BlockSpec(memory_space=pl.ANY)
    return pl.pallas_call(
        functools.partial(_mlstm_kernel, L=L),
        grid=(b // G, t // tt),
        in_specs=[col(4), col(5), col(6), col(7),
                  pl.BlockSpec((G, tt, N_SMALL), lambda i, j: (i, j, COL_SMALL // N_SMALL)),
                  pl.BlockSpec((G, tt // L, 2 * N_HEAD, L), lambda i, j: (i, j, 0, 0)),
                  full2((1, N_SMALL)), full2((2 * N_HEAD, 1)), full2((1, D_BR))]
                 + [_layer_block(s, l_in, 2) for s in shapes] + [any_spec] * 3,
        out_specs=[pl.BlockSpec((G, tt, D_BR), lambda i, j: (i, j, 0))] + [_layer_block(s, l, 2) for s in shapes],
        out_shape=[jax.ShapeDtypeStruct((b, t, D_BR), BF16)]
                  + [jax.ShapeDtypeStruct((depth, b) + s[1:], F32) for s in shapes],
        input_output_aliases=aliases,
        compiler_params=_cparams(("parallel", "arbitrary")),
        name="mlstm",
    )(proj3, proj3, proj3, proj3, proj3, small_row, bias_c, bias_r, nw, *st_in, *st_prev)


PAD_C = 8
PAD_D = 32
CONV_ROWS = 32


def _conv_kernel(bc_ref, cc_ref, xc_ref, ad_ref, gd_ref, bufc_ref, bufd_ref, wc_ref, wd_ref, bd_ref,
                 lg_ref, lbias_ref, pc_ref, pd_ref, oc_ref, od_ref, nbc_ref, nbd_ref, extc, extd, shc, shd):
    del pc_ref, pd_ref
    bblk, tt, _ = bc_ref.shape
    hc, hd = CONV_C - 1, CONV_D - 1
    rt = min(tt, CONV_ROWS)
    offs_c = [PAD_C - hc + j for j in range(CONV_C)]
    offs_d = [PAD_D - hd + j for j in range(CONV_D)]
    unaligned_c = [o for o in offs_c if o % SUBLANES]
    span_d = tt + PAD_D - SUBLANES

    @pl.when(pl.program_id(1) == 0)
    def _():
        extc[:, PAD_C - hc:PAD_C, :] = bufc_ref[...]
        extd[:, PAD_D - hd:PAD_D, :] = bufd_ref[...]

    extc[:, PAD_C:PAD_C + tt, :] = cc_ref[...] * xc_ref[...]
    extd[:, PAD_D:PAD_D + tt, :] = ad_ref[...] * _sigmoid(gd_ref[...])

    def tile_rows(w8):
        return w8 if rt == SUBLANES else jnp.concatenate([w8] * (rt // SUBLANES), axis=0)

    def seq_body(bi, carry):
        for res in range(1, SUBLANES):
            shd[res, 0:span_d, :] = extd[bi, res:res + span_d, :]
        for i, o in enumerate(unaligned_c):
            shc[i, 0:tt, :] = extc[bi, o:o + tt, :]

        def tile_body(ti, c2):
            r0 = pl.multiple_of(ti * rt, rt)
            acc = jnp.zeros((rt, D_BR), F32)
            for j, o in enumerate(offs_c):
                if o % SUBLANES:
                    x = shc[unaligned_c.index(o), pl.ds(r0, rt), :]
                else:
                    x = extc[bi, pl.ds(pl.multiple_of(r0 + o, SUBLANES), rt), :]
                acc = acc + tile_rows(wc_ref[j]) * x
            oc_ref[bi, pl.ds(r0, rt), :] = (bc_ref[bi, pl.ds(r0, rt), :] * acc).astype(oc_ref.dtype)
            acc = jnp.zeros((rt, D_BR), F32)
            for j, o in enumerate(offs_d):
                res = o % SUBLANES
                rows = pl.ds(pl.multiple_of(r0 + (o - res), SUBLANES), rt)
                x = shd[res, rows, :] if res else extd[bi, rows, :]
                acc = acc + tile_rows(wd_ref[j]) * x
            y = acc + bd_ref[...]
            yc = y - jnp.mean(y, axis=-1, keepdims=True)
            z = yc * lax.rsqrt(jnp.mean(yc * yc, axis=-1, keepdims=True) + EPS) * lg_ref[...] + lbias_ref[...]
            od_ref[bi, pl.ds(r0, rt), :] = (z * _sigmoid(z)).astype(od_ref.dtype)
            return c2

        lax.fori_loop(0, tt // rt, tile_body, 0, unroll=4 if (tt // rt) % 4 == 0 else 1)
        return carry

    lax.fori_loop(0, bblk, seq_body, 0)

    new_c = extc[:, PAD_C + tt - hc:PAD_C + tt, :]
    new_d = extd[:, PAD_D + tt - hd:PAD_D + tt, :]
    nbc_ref[...] = new_c
    nbd_ref[...] = new_d
    extc[:, PAD_C - hc:PAD_C, :] = new_c
    extd[:, PAD_D - hd:PAD_D, :] = new_d


def _conv(proj3, buf_in, l_in, buf_prev, l, depth, wc, wd, bd, lg, lbias, bblk, tt):
    b, t, _ = proj3.shape
    c0 = COL_CONV // D_BR
    col = lambda c: pl.BlockSpec((bblk, tt, D_BR), lambda i, j: (i, j, c0 + c))
    full2 = lambda shape: pl.BlockSpec(shape, lambda i, j: (0, 0))
    full3 = lambda shape: pl.BlockSpec(shape, lambda i, j: (0, 0, 0))
    rows8 = lambda w: jnp.broadcast_to(w[:, None, :], (w.shape[0], SUBLANES, D_BR))
    shapes = [(bblk, CONV_C - 1, D_BR), (bblk, CONV_D - 1, D_BR)]
    outc =pl.BlockSpec((bblk, tt, D_BR), lambda i, j: (i, j, 0))
    aliases = {}
    if buf_prev is None:
        buf_prev = [jnp.zeros((1,), F32)] * 2
    else:
        aliases = {12: 2, 13: 3}
    any_spec = pl.BlockSpec(memory_space=pl.ANY)
    return pl.pallas_call(
        _conv_kernel,
        grid=(b // bblk, t // tt),
        in_specs=[col(0), col(1), col(2), col(3), col(4)] + [_layer_block(s, l_in, 2) for s in shapes]
                 + [full3((CONV_C, SUBLANES, D_BR)), full3((CONV_D, SUBLANES, D_BR)), full2((1, D_BR)),
                    full2((1, D_BR)), full2((1, D_BR)), any_spec, any_spec],
        out_specs=[outc, outc] + [_layer_block(s, l, 2) for s in shapes],
        out_shape=[jax.ShapeDtypeStruct((b, t, D_BR), BF16), jax.ShapeDtypeStruct((b, t, D_BR), BF16)]
                  + [jax.ShapeDtypeStruct((depth, b) + s[1:], F32) for s in shapes],
        scratch_shapes=[pltpu.VMEM((bblk, PAD_C + tt, D_BR), F32), pltpu.VMEM((bblk, PAD_D + tt, D_BR), F32),
                        pltpu.VMEM((CONV_C - 1, tt, D_BR), F32), pltpu.VMEM((SUBLANES, PAD_D + tt, D_BR), F32)],
        input_output_aliases=aliases,
        compiler_params=_cparams(("parallel", "arbitrary")),
        name="conv",
    )(proj3, proj3, proj3, proj3, proj3, *buf_in, rows8(wc), rows8(wd), bd, lg, lbias, *buf_prev)


FF_TILE = 1024
MIX_VMEM_LIMIT = 60 * 1024 * 1024


def _mix_kernel(x_ref, oa_ref, ob_ref, oc_ref, od_ref, gt_ref, wb_ref, wo_ref, nm_ref, wu_ref, wdn_ref,
                nn_ref, *out_refs, last):
    merged = None
    for n, br in enumerate((oa_ref, ob_ref, oc_ref, od_ref)):
        gate = _sigmoid(gt_ref[:, n * D_MODEL:(n + 1) * D_MODEL])
        term = gate * jnp.dot(br[...], wb_ref[n], preferred_element_type=F32)
        merged = term if merged is None else merged + term
    x = x_ref[...] + jnp.dot(merged.astype(BF16), wo_ref[...], preferred_element_type=F32)
    hm = _rms(x, nm_ref[...]).astype(BF16)
    for c in range(D_FF // FF_TILE):
        cs = slice(c * FF_TILE, (c + 1) * FF_TILE)
        up = jnp.maximum(jnp.dot(hm, wu_ref[:, cs], preferred_element_type=F32), 0.0)
        x = x + jnp.dot((up * up).astype(BF16), wdn_ref[cs, :], preferred_element_type=F32)
    if last:
        out_refs[0][...] = _rms(x, nn_ref[...])
    else:
        out_refs[0][...] = x
        out_refs[1][...] = _rms(x, nn_ref[...]).astype(BF16)


def _mix(x, oa, ob, oc, od, proj, wb, wo, nm, wu, wdn, nn, l, last):
    n = x.shape[0]
    tm = min(n, ROW_TILE)
    row = lambda w: pl.BlockSpec((tm, w), lambda i: (i, 0))
    once = pl.Buffered(1)
    full2 = lambda shape: pl.BlockSpec(shape, lambda i: (0, 0))
    layer3 = lambda shape: pl.BlockSpec((None,) + shape, lambda i: (l, 0, 0), pipeline_mode=once)
    out_specs = [row(D_MODEL)] if last else [row(D_MODEL), row(D_MODEL)]
    out_shape = [jax.ShapeDtypeStruct((n, D_MODEL), F32)]
    if not last:
        out_shape.append(jax.ShapeDtypeStruct((n, D_MODEL), BF16))
    return pl.pallas_call(
        functools.partial(_mix_kernel, last=last),
        grid=(n // tm,),
        in_specs=[row(D_MODEL), row(D_BR), row(D_BR), row(D_BR), row(D_BR),
                  pl.BlockSpec((tm, N_GATE), lambda i: (i, COL_GATE // N_GATE)),
                  pl.BlockSpec((None, N_BRANCH, D_BR, D_MODEL), lambda i: (l, 0, 0, 0), pipeline_mode=once),
                  layer3((D_MODEL, D_MODEL)), full2((1, D_MODEL)), layer3((D_MODEL, D_FF)),
                  layer3((D_FF, D_MODEL)), full2((1, D_MODEL))],
        out_specs=out_specs,
        out_shape=out_shape,
        compiler_params=_cparams(("parallel",), MIX_VMEM_LIMIT),
        name="mix",
    )(x, oa, ob, oc, od, proj, wb, wo, nm, wu, wdn, nn)


def _layer(x, h, b, t, st_in, l_in, st_prev, l, depth, lw, wts, last):
    (lb, n_hgrn, bias_c, bias_r, n_mlstm, wc, wd, bd, lg, lbias, n_mlp, n_next) = lw
    w_in, wb, wo, wu, wdn = wts
    L = math.gcd(t, CHUNK)
    G = min(b, SEQ_GROUP)
    proj = _proj(h, w_in, l)
    proj3 = proj.reshape(b, t, D_PROJ)
    small = proj3[:, :, COL_SMALL:COL_SMALL + 2 * N_HEAD]
    small_row = jnp.swapaxes(small.reshape(b, t // L, L, 2 * N_HEAD), 2, 3)

    prev = (None,) * 6 if st_prev is None else st_prev
    g_h, tt_h = (2, 512) if t >= 512 else (G, t)
    oa, s_hgrn = _hgrn(proj3, lb, n_hgrn, st_in[0], l_in, prev[0], l, depth, L, g_h, tt_h)
    tt_m = min(t, 128)
    ob, s_c, s_n, s_m = _mlstm(proj3, small_row, bias_c, bias_r, n_mlstm, st_in[1:4], l_in,
                               None if st_prev is None else prev[1:4], l, depth, L, G, tt_m)
    g_c, tt_c = (1, 512) if t >= 512 else (G, t)
    oc, od, buf_c, buf_d = _conv(proj3, st_in[4:6], l_in, None if st_prev is None else prev[4:6], l, depth,
                                 wc, wd, bd, lg, lbias, g_c, tt_c)
    flat = lambda a: a.reshape(b * t, D_BR)
    outs = _mix(x, flat(oa), flat(ob), flat(oc), flat(od), proj, wb, wo, n_mlp, wu, wdn, n_next, l, last)
    return outs, (s_hgrn, s_c, s_n, s_m, buf_c, buf_d)


def kernel(x_prompt, x_sample, state_hgrn, state_mlstm_c, state_mlstm_n, state_mlstm_m, state_conv_short, state_conv_conformer, w_in, w_branch, w_out, lb_logits, norm_hgrn, mlstm_i_bias, mlstm_f_bias, norm_mlstm, conv_short_w, conv_conformer_w, conv_conformer_b, ln_conformer_g, ln_conformer_b, norm_mix, norm_mlp, w_up, w_down, norm_final):
    depth = w_in.shape[0]
    bp, tp, _ = x_prompt.shape
    bs, ts, _ = x_sample.shape
    lower_bounds = _lower_bounds(lb_logits)

    o_small = N_MAIN
    o_conv = N_MAIN + 2 * N_HEAD
    o_gate = o_conv + N_CONV
    w_in_p = jnp.concatenate(
        [w_in[:, :, :N_MAIN], w_in[:, :, o_gate:o_gate + N_GATE], w_in[:, :, o_conv:o_conv + N_CONV],
         w_in[:, :, o_small:o_small + 2 * N_HEAD],
         jnp.zeros((depth, D_MODEL, N_SMALL - 2 * N_HEAD), w_in.dtype)], axis=2).astype(BF16)
    wts = (w_in_p,) + tuple(a.astype(BF16) for a in (w_branch, w_out, w_up, w_down))
    gate_bias = jnp.concatenate([mlstm_i_bias, mlstm_f_bias], axis=1).astype(F32)
    bias_c = jnp.pad(gate_bias, ((0, 0), (0, N_SMALL - 2 * N_HEAD)))[:, None, :]
    bias_r = gate_bias[:, :, None]
    row = lambda a, l: a[l][None, :].astype(F32)

    zero_st = (jnp.zeros((1, bp, N_HEAD, D_HEAD, D_HEAD), F32), jnp.zeros((1, bp, N_HEAD, D_HEAD, D_HEAD), F32),
               jnp.zeros((1, bp, N_HEAD, 1, D_HEAD), F32), jnp.zeros((1, bp, 1, N_SMALL), F32),
               jnp.zeros((1, bp, CONV_C - 1, D_BR), F32), jnp.zeros((1, bp, CONV_D - 1, D_BR), F32))
    m_lanes = jnp.pad(state_mlstm_m.astype(F32), ((0, 0), (0, 0), (0, N_SMALL - N_HEAD)))[:, :, None, :]
    past_st = (state_hgrn, state_mlstm_c, state_mlstm_n[:, :, :, None, :], m_lanes,
               state_conv_short, state_conv_conformer)

    xp = x_prompt.reshape(bp * tp, D_MODEL)
    xs = x_sample.reshape(bs * ts, D_MODEL)
    hp = _norm(xp, row(norm_mix, 0))
    hs = _norm(xs, row(norm_mix, 0))
    st_p = st_s = None
    for l in range(depth):
        last = l == depth - 1
        n_next = norm_final[None, :].astype(F32) if last else row(norm_mix, l + 1)
        lw = (lower_bounds[l][None, :], row(norm_hgrn, l), bias_c[l], bias_r[l], row(norm_mlstm, l),
              conv_short_w[l], conv_conformer_w[l], row(conv_conformer_b, l), row(ln_conformer_g, l),
              row(ln_conformer_b, l), row(norm_mlp, l), n_next)
        outs_p, st_p = _layer(xp, hp, bp, tp, zero_st, 0, st_p, l, depth, lw, wts, last)
        outs_s, st_s = _layer(xs, hs, bs, ts, past_st, l, st_s, l, depth, lw, wts, last)
        if last:
            xp, xs = outs_p[0], outs_s[0]
        else:
            (xp, hp), (xs, hs) = outs_p, outs_s

    def finish(x, b, t, st):
        s_hgrn, s_c, s_n, s_m, buf_c, buf_d = st
        return (x.reshape(b, t, D_MODEL), s_hgrn, s_c, s_n[:, :, :, 0, :], s_m[:, :, 0, :N_HEAD], buf_c, buf_d)

    fp = finish(xp, bp, tp, st_p)
    fs = finish(xs, bs, ts, st_s)
    outs = []
    for a, c in zip(fp, fs):
        outs += [a, c]
    return tuple(outs)
```

```python
import functools
import math

import jax
import jax.numpy as jnp
from jax import lax
from jax.experimental import pallas as pl
from jax.experimental.pallas import tpu as pltpu

F32 = jnp.float32
BF16 = jnp.bfloat16

D_MODEL = 1024
N_HEAD = 4
D_HEAD = 128
D_BR = 512
N_BRANCH = 4
CONV_C = 3
CONV_D = 31
D_FF = 4 * D_MODEL
CHUNK = 64
EPS = 1e-6
SUBLANES = 8
LANES = 128

N_MAIN = 8 * D_BR
N_GATE = N_BRANCH * D_MODEL
N_CONV = 5 * D_BR
N_SMALL = LANES
D_PROJ = N_MAIN + N_GATE + N_CONV + N_SMALL
COL_GATE = N_MAIN
COL_CONV = N_MAIN + N_GATE
COL_SMALL = N_MAIN + N_GATE + N_CONV
PROJ_TN = D_PROJ // 5

VMEM_LIMIT = 56 * 1024 * 1024
ROW_TILE = 512
SEQ_GROUP = 8


def _cparams(sem, vmem_limit=VMEM_LIMIT):
    return pltpu.CompilerParams(dimension_semantics=sem, vmem_limit_bytes=vmem_limit)


def _dot(a, b):
    return jnp.dot(a.astype(BF16), b.astype(BF16), preferred_element_type=F32)


def _dot_nt(a, b):
    return lax.dot_general(a.astype(BF16), b.astype(BF16), (((1,), (1,)), ((), ())),
                           preferred_element_type=F32)


def _dot_tn(a, b):
    return lax.dot_general(a.astype(BF16), b.astype(BF16), (((0,), (0,)), ((), ())),
                           preferred_element_type=F32)


def _split3(x):
    hi = x.astype(BF16)
    r = x - hi.astype(F32)
    mid = r.astype(BF16)
    lo = (r - mid.astype(F32)).astype(BF16)
    return hi, mid, lo


def _sigmoid(x):
    return jax.nn.sigmoid(x)


def _log_sigmoid(x):
    return jnp.minimum(x, 0.0) - jnp.log1p(jnp.exp(-jnp.abs(x)))


def _rms(x, g):
    return x * lax.rsqrt(jnp.mean(x * x, axis=-1, keepdims=True) + EPS) * g


def _layer_block(shape, l, grid_rank):
    zeros = (0,) * (len(shape) - 1)
    if grid_rank == 2:
        return pl.BlockSpec((None,) + shape, lambda i, j: (l, i) + zeros)
    return pl.BlockSpec((None,) + shape, lambda i: (l, i) + zeros)


def _lb_kernel(x_ref, o_ref):
    x = x_ref[...]
    depth = x.shape[0]
    e = jnp.exp(x - jnp.max(x, axis=0, keepdims=True))
    s = e / jnp.sum(e, axis=0, keepdims=True)
    acc = jnp.zeros_like(s[0:1])
    for i in range(depth):
        acc = acc + s[i:i + 1]
        o_ref[i:i + 1, :] = acc - s[0:1]


def _lower_bounds(lb_logits):
    return pl.pallas_call(
        _lb_kernel, out_shape=jax.ShapeDtypeStruct(lb_logits.shape, F32), name="lower_bounds",
    )(lb_logits.astype(F32))


def _norm_kernel(x_ref, g_ref, o_ref):
    o_ref[...] = _rms(x_ref[...], g_ref[...]).astype(o_ref.dtype)


def _norm(x, g):
    n = x.shape[0]
    tm = min(n, ROW_TILE)
    return pl.pallas_call(
        _norm_kernel,
        grid=(n // tm,),
        in_specs=[pl.BlockSpec((tm, D_MODEL), lambda i: (i, 0)), pl.BlockSpec((1, D_MODEL), lambda i: (0, 0))],
        out_specs=pl.BlockSpec((tm, D_MODEL), lambda i: (i, 0)),
        out_shape=jax.ShapeDtypeStruct((n, D_MODEL), BF16),
        compiler_params=_cparams(("parallel",)),
        name="norm",
    )(x, g)


PERM_ROWS = 128


def _permute_kernel(w_ref, o_ref):
    x = w_ref[...]
    o_small = N_MAIN
    o_conv = o_small + 2 * N_HEAD
    o_gate = o_conv + N_CONV
    pad = jnp.zeros((x.shape[0], N_SMALL - 2 * N_HEAD), x.dtype)
    o_ref[:, :N_MAIN] = x[:, :N_MAIN].astype(o_ref.dtype)
    o_ref[:, COL_GATE:COL_GATE + N_GATE] = x[:, o_gate:o_gate + N_GATE].astype(o_ref.dtype)
    o_ref[:, COL_CONV:COL_CONV + N_CONV] = x[:, o_conv:o_conv + N_CONV].astype(o_ref.dtype)
    o_ref[:, COL_SMALL:] = jnp.concatenate([x[:, o_small:o_conv], pad], axis=1).astype(o_ref.dtype)


def _permute_w_in(w_in):
    depth, d, d_in = w_in.shape
    return pl.pallas_call(
        _permute_kernel,
        grid=(depth, d // PERM_ROWS),
        in_specs=[pl.BlockSpec((None, PERM_ROWS, d_in), lambda l, i: (l, i, 0))],
        out_specs=pl.BlockSpec((None, PERM_ROWS, D_PROJ), lambda l, i: (l, i, 0)),
        out_shape=jax.ShapeDtypeStruct((depth, d, D_PROJ), BF16),
        compiler_params=_cparams(("parallel", "parallel")),
        name="permute_w_in",
    )(w_in)


def _proj_kernel(h_ref, w_ref, o_ref):
    o_ref[...] = jnp.dot(h_ref[...], w_ref[...], preferred_element_type=F32)


def _proj(h, w, l):
    n = h.shape[0]
    tm = min(n, 2 * ROW_TILE)
    return pl.pallas_call(
        _proj_kernel,
        grid=(D_PROJ // PROJ_TN, n // tm),
        in_specs=[pl.BlockSpec((tm, D_MODEL), lambda j, i: (i, 0)),
                  pl.BlockSpec((None, D_MODEL, PROJ_TN), lambda j, i: (l, 0, j))],
        out_specs=pl.BlockSpec((tm, PROJ_TN), lambda j, i: (i, j)),
        out_shape=jax.ShapeDtypeStruct((n, D_PROJ), F32),
        compiler_params=_cparams(("parallel", "parallel")),
        name="proj",
    )(h, w)


def _cumsum_rows(x):
    n = x.shape[0]
    row = lax.broadcasted_iota(jnp.int32, x.shape, 0)
    s = 1
    while s < n:
        x = x + jnp.where(row >= s, pltpu.roll(x, s, 0), 0.0)
        s *= 2
    return x


def _group_row(x, group, j):
    n, w = x.shape
    x3 = x.reshape(n // group, group, w)
    return jnp.broadcast_to(x3[:, j:j + 1, :], x3.shape).reshape(n, w)


def _boundary_row(x, c, row):
    if 2 * c >= SUBLANES:
        return _group_row(x, 2 * c, c - 1)
    if c == 2:
        return jnp.where(row % SUBLANES < 4, _group_row(x, SUBLANES, 1), _group_row(x, SUBLANES, 5))
    return jnp.where(row % 2 == 1, pltpu.roll(x, 1, 0), x)


def _hgrn_kernel(qa_ref, fa_ref, ia_ref, ga_ref, lb_ref, nw_ref, s0_ref, prev_ref, o_ref, s_ref, *, L):
    del prev_ref
    G, tt, _ = qa_ref.shape
    nchunk = tt // L
    levels = [c for c in (32, 16, 8, 4, 2, 1) if 2 * c <= L]

    @pl.when(pl.program_id(1) == 0)
    def _():
        s_ref[...] = s0_ref[...]

    lb = lb_ref[...]
    nw = nw_ref[...]
    row = lax.broadcasted_iota(jnp.int32, (L, D_BR), 0)
    row_l = lax.broadcasted_iota(jnp.int32, (L, L), 0)
    col_l = lax.broadcasted_iota(jnp.int32, (L, L), 1)
    heads = [slice(h * D_HEAD, (h + 1) * D_HEAD) for h in range(N_HEAD)]
    level = jnp.where(row_l == col_l, 0.0, -1.0)
    for c in levels:
        pair = ((row_l // (2 * c)) == (col_l // (2 * c))) & (row_l % (2 * c) >= c) & (col_l % (2 * c) < c)
        level = jnp.where(pair, float(c), level)

    def body(ci, carry):
        rows = pl.ds(pl.multiple_of(ci * L, L), L)
        seqs = range(G)
        k, q, gc, out, scores = {}, {}, {}, {}, {}
        for bi in seqs:
            f = lb + (1.0 - lb) * _sigmoid(fa_ref[bi, rows, :])
            k[bi] = 1.0 - f
            qa = qa_ref[bi, rows, :]
            q[bi] = qa * _sigmoid(qa)
            gc[bi] = _cumsum_rows(jnp.log(f))
        for bi in seqs:
            qg = q[bi] * jnp.exp(gc[bi])
            for h, hs in enumerate(heads):
                out[bi, h] = _dot(qg[:, hs], s_ref[bi, h])
                scores[bi, h] = jnp.where(level == 0.0, _dot_nt(q[bi][:, hs], k[bi][:, hs]), 0.0)
        for c in levels:
            for bi in seqs:
                diff = gc[bi] - _boundary_row(gc[bi], c, row)
                qe = q[bi] * jnp.exp(diff)
                ke = k[bi] * jnp.exp(-diff)
                for h, hs in enumerate(heads):
                    scores[bi, h] = jnp.where(level == float(c), _dot_nt(qe[:, hs], ke[:, hs]), scores[bi, h])
        for bi in seqs:
            v = ia_ref[bi, rows, :]
            g_last = gc[bi][L - 1:L, :]
            k_dec = k[bi] * jnp.exp(g_last - gc[bi])
            decay = jnp.exp(g_last)
            ga = ga_ref[bi, rows, :]
            gate = ga * _sigmoid(ga)
            for h, hs in enumerate(heads):
                o = out[bi, h] + _dot(scores[bi, h], v[:, hs])
                decay_col = jnp.broadcast_to(decay[:, hs], (D_HEAD, D_HEAD)).T
                s_ref[bi, h] = decay_col * s_ref[bi, h] + _dot_tn(k_dec[:, hs], v[:, hs])
                y = o * lax.rsqrt(jnp.mean(o * o, axis=-1, keepdims=True) + EPS) * nw[:, hs]
                o_ref[bi, rows, hs] = (y * gate[:, hs]).astype(o_ref.dtype)
        return carry

    lax.fori_loop(0, nchunk, body, 0)


def _hgrn(proj3, lb, nw, s_in, l_in, s_prev, l, depth, L, G, tt):
    b, t, _ = proj3.shape
    col = lambda c: pl.BlockSpec((G, tt, D_BR), lambda i, j: (i, j, c))
    vec = pl.BlockSpec((1, D_BR), lambda i, j: (0, 0))
    st = (G, N_HEAD, D_HEAD, D_HEAD)
    aliases = {}
    if s_prev is None:
        s_prev = jnp.zeros((1,), F32)
    else:
        aliases = {7: 1}
    return pl.pallas_call(
        functools.partial(_hgrn_kernel, L=L),
        grid=(b // G, t // tt),
        in_specs=[col(0), col(1), col(2), col(3), vec, vec, _layer_block(st, l_in, 2),
                  pl.BlockSpec(memory_space=pl.ANY)],
        out_specs=[pl.BlockSpec((G, tt, D_BR), lambda i, j: (i, j, 0)), _layer_block(st, l, 2)],
        out_shape=[jax.ShapeDtypeStruct((b, t, D_BR), BF16),
                   jax.ShapeDtypeStruct((depth, b) + st[1:], F32)],
        input_output_aliases=aliases,
        compiler_params=_cparams(("parallel", "arbitrary")),
        name="hgrn",
    )(proj3, proj3, proj3, proj3, lb, nw, s_in, s_prev)


def _mlstm_kernel(q_ref, k_ref, v_ref, og_ref, sc_ref, sr_ref, bc_ref, br_ref, nw_ref,
                  c0_ref, n0_ref, m0_ref, cp_ref, np_ref, mp_ref, o_ref, c_ref, n_ref, m_ref, *, L):
    del cp_ref, np_ref, mp_ref
    G, tt, _ = q_ref.shape
    nchunk = tt // L

    @pl.when(pl.program_id(1) == 0)
    def _():
        c_ref[...] = c0_ref[...]
        n_ref[...] = n0_ref[...]
        m_ref[...] = m0_ref[...]

    nw = nw_ref[...]
    bias_c = bc_ref[...]
    bias_r = br_ref[...]
    row_l = lax.broadcasted_iota(jnp.int32, (L, L), 0)
    col_l = lax.broadcasted_iota(jnp.int32, (L, L), 1)
    causal = row_l >= col_l
    tri = jnp.where(causal, 1.0, 0.0).astype(BF16)
    tri_t = jnp.where(row_l <= col_l, 1.0, 0.0).astype(BF16)
    row_c = lax.broadcasted_iota(jnp.int32, (L, N_SMALL), 0)
    ones = jnp.ones((L, D_HEAD), F32)
    scale = D_HEAD ** -0.5
    heads = [slice(h * D_HEAD, (h + 1) * D_HEAD) for h in range(N_HEAD)]
    group = G

    def gates(bi, ci, rows):
        pre_c = sc_ref[bi, rows, :] + bias_c
        pre_r = sr_ref[bi, ci] + bias_r
        hi, mid, lo = _split3(_log_sigmoid(pre_c))
        b_c = (jnp.dot(tri, hi, preferred_element_type=F32) + jnp.dot(tri, mid, preferred_element_type=F32)
               + jnp.dot(tri, lo, preferred_element_type=F32))
        hi, mid, lo = _split3(_log_sigmoid(pre_r))
        b_r = (jnp.dot(hi, tri_t, preferred_element_type=F32) + jnp.dot(mid, tri_t, preferred_element_type=F32)
               + jnp.dot(lo, tri_t, preferred_element_type=F32))
        b = pltpu.roll(b_c, N_SMALL - N_HEAD, 1)
        a = pre_c - b
        cm = a
        s = 1
        while s < L:
            cm = jnp.maximum(cm, jnp.where(row_c >= s, pltpu.roll(cm, s, 0), -jnp.inf))
            s *= 2
        m_old = m_ref[bi]
        big_m = jnp.maximum(m_old, cm)
        m_last = big_m[L - 1:L, :]
        m_ref[bi] = b[L - 1:L, :] + m_last
        return dict(a_r=pre_r[0:N_HEAD, :] - b_r[N_HEAD:2 * N_HEAD, :], big_m=big_m, m_old=m_old,
                    w_inter=jnp.exp(m_old - big_m), n_exp=jnp.exp(-(b + big_m)),
                    w_state=jnp.exp(a - m_last), carry=jnp.exp(m_old - m_last))

    def body(ci, carry):
        rows = pl.ds(pl.multiple_of(ci * L, L), L)
        for g0 in range(0, G, group):
            seqs = range(g0, g0 + group)
            sh = {bi: gates(bi, ci, rows) for bi in seqs}
            probs = [(bi, h) for bi in seqs for h in range(N_HEAD)]
            qk, wi_b, intra, qc, qn = {}, {}, {}, {}, {}
            for p in probs:
                bi, h = p
                g = sh[bi]
                m_b = jnp.broadcast_to(g["big_m"][:, h:h + 1], (L, D_HEAD))
                wi_b[p] = jnp.exp(g["m_old"][:, h:h + 1] - m_b)
                w = jnp.where(causal, jnp.exp(g["a_r"][h:h + 1, :] - m_b[:, :L]), 0.0)
                q = q_ref[bi, rows, heads[h]] * scale
                qk[p] = _dot_nt(q, k_ref[bi, rows, heads[h]]) * w
            for p in probs:
                bi, h = p
                q = q_ref[bi, rows, heads[h]] * scale
                intra[p] = _dot(qk[p], jnp.concatenate([v_ref[bi, rows, heads[h]], ones], axis=1))
                qc[p] = _dot(q, c_ref[bi, h])
                qn[p] = _dot_nt(q, jnp.broadcast_to(n_ref[bi, h], (D_HEAD, D_HEAD)))
            for p in probs:
                bi, h = p
                g = sh[bi]
                num = intra[p][:, :D_HEAD] + wi_b[p] * qc[p]
                den = intra[p][:, D_HEAD + h:D_HEAD + h + 1] + g["w_inter"][:, h:h + 1] * qn[p][:, h:h + 1]
                r = 1.0 / jnp.maximum(jnp.abs(den), g["n_exp"][:, h:h + 1])
                t = r * lax.rsqrt(r * r * jnp.mean(num * num, axis=-1, keepdims=True) + EPS)
                gate = _sigmoid(og_ref[bi, rows, heads[h]])
                o_ref[bi, rows, heads[h]] = (num * t * nw[:, heads[h]] * gate).astype(o_ref.dtype)
            for p in probs:
                bi, h = p
                g = sh[bi]
                k = k_ref[bi, rows, heads[h]]
                kw = g["w_state"][:, h:h + 1] * k
                cd = g["carry"][:, h:h + 1]
                c_ref[bi, h] = cd * c_ref[bi, h] + _dot_tn(kw, v_ref[bi, rows, heads[h]])
                n_ref[bi, h] = cd * n_ref[bi, h] + jnp.sum(kw, axis=0, keepdims=True)
        return carry

    lax.fori_loop(0, nchunk, body, 0)


def _mlstm(proj3, small_row, bias_c, bias_r, nw, st_in, l_in, st_prev, l, depth, L, G, tt):
    b, t, _ = proj3.shape
    col = lambda c: pl.BlockSpec((G, tt, D_BR), lambda i, j: (i, j, c))
    full2 = lambda shape: pl.BlockSpec(shape, lambda i, j: (0, 0))
    shapes = [(G, N_HEAD, D_HEAD, D_HEAD), (G, N_HEAD, 1, D_HEAD), (G, 1, N_SMALL)]
    aliases = {}
    if st_prev is None:
        st_prev = [jnp.zeros((1,), F32)] * 3
    else:
        aliases = {12: 1, 13: 2, 14: 3}
    any_spec = pl.BlockSpec(memory_space=pl.ANY)
    return pl.pallas_call(
        functools.partial(_mlstm_kernel, L=L),
        grid=(b // G, t // tt),
        in_specs=[col(4), col(5), col(6), col(7),
                  pl.BlockSpec((G, tt, N_SMALL), lambda i, j: (i, j, COL_SMALL // N_SMALL)),
                  pl.BlockSpec((G, tt // L, 2 * N_HEAD, L), lambda i, j: (i, j, 0, 0)),
                  full2((1, N_SMALL)), full2((2 * N_HEAD, 1)), full2((1, D_BR))]
                 + [_layer_block(s, l_in, 2) for s in shapes] + [any_spec] * 3,
        out_specs=[pl.BlockSpec((G, tt, D_BR), lambda i, j: (i, j, 0))] + [_layer_block(s, l, 2) for s in shapes],
        out_shape=[jax.ShapeDtypeStruct((b, t, D_BR), BF16)]
                  + [jax.ShapeDtypeStruct((depth, b) + s[1:], F32) for s in shapes],
        input_output_aliases=aliases,
        compiler_params=_cparams(("parallel", "arbitrary")),
        name="mlstm",
    )(proj3, proj3, proj3, proj3, proj3, small_row, bias_c, bias_r, nw, *st_in, *st_prev)


PAD_C = 8
PAD_D = 32
CONV_ROWS = 32


def _conv_kernel(bc_ref, cc_ref, xc_ref, ad_ref, gd_ref, bufc_ref, bufd_ref, wc_ref, wd_ref, bd_ref,
                 lg_ref, lbias_ref, pc_ref, pd_ref, oc_ref, od_ref, nbc_ref, nbd_ref, extc, extd, shc, shd):
    del pc_ref, pd_ref
    bblk, tt, _ = bc_ref.shape
    hc, hd = CONV_C - 1, CONV_D - 1
    rt = min(tt, CONV_ROWS)
    offs_c = [PAD_C - hc + j for j in range(CONV_C)]
    offs_d = [PAD_D - hd + j for j in range(CONV_D)]
    unaligned_c = [o for o in offs_c if o % SUBLANES]
    span_d = tt + PAD_D - SUBLANES

    @pl.when(pl.program_id(1) == 0)
    def _():
        extc[:, PAD_C - hc:PAD_C, :] = bufc_ref[...]
        extd[:, PAD_D - hd:PAD_D, :] = bufd_ref[...]

    extc[:, PAD_C:PAD_C + tt, :] = cc_ref[...] * xc_ref[...]
    extd[:, PAD_D:PAD_D + tt, :] = ad_ref[...] * _sigmoid(gd_ref[...])

    def tile_rows(w8):
        return w8 if rt == SUBLANES else jnp.concatenate([w8] * (rt // SUBLANES), axis=0)

    def seq_body(bi, carry):
        for res in range(1, SUBLANES):
            shd[res, 0:span_d, :] = extd[bi, res:res + span_d, :]
        for i, o in enumerate(unaligned_c):
            shc[i, 0:tt, :] = extc[bi, o:o + tt, :]

        def tile_body(ti, c2):
            r0 = pl.multiple_of(ti * rt, rt)
            acc = jnp.zeros((rt, D_BR), F32)
            for j, o in enumerate(offs_c):
                if o % SUBLANES:
                    x = shc[unaligned_c.index(o), pl.ds(r0, rt), :]
                else:
                    x = extc[bi, pl.ds(pl.multiple_of(r0 + o, SUBLANES), rt), :]
                acc = acc + tile_rows(wc_ref[j]) * x
            oc_ref[bi, pl.ds(r0, rt), :] = (bc_ref[bi, pl.ds(r0, rt), :] * acc).astype(oc_ref.dtype)
            acc = jnp.zeros((rt, D_BR), F32)
            for j, o in enumerate(offs_d):
                res = o % SUBLANES
                rows = pl.ds(pl.multiple_of(r0 + (o - res), SUBLANES), rt)
                x = shd[res, rows, :] if res else extd[bi, rows, :]
                acc = acc + tile_rows(wd_ref[j]) * x
            y = acc + bd_ref[...]
            yc = y - jnp.mean(y, axis=-1, keepdims=True)
            z = yc * lax.rsqrt(jnp.mean(yc * yc, axis=-1, keepdims=True) + EPS) * lg_ref[...] + lbias_ref[...]
            od_ref[bi, pl.ds(r0, rt), :] = (z * _sigmoid(z)).astype(od_ref.dtype)
            return c2

        lax.fori_loop(0, tt // rt, tile_body, 0, unroll=4 if (tt // rt) % 4 == 0 else 1)
        return carry

    lax.fori_loop(0, bblk, seq_body, 0)

    new_c = extc[:, PAD_C + tt - hc:PAD_C + tt, :]
    new_d = extd[:, PAD_D + tt - hd:PAD_D + tt, :]
    nbc_ref[...] = new_c
    nbd_ref[...] = new_d
    extc[:, PAD_C - hc:PAD_C, :] = new_c
    extd[:, PAD_D - hd:PAD_D, :] = new_d


def _conv(proj3, buf_in, l_in, buf_prev, l, depth, wc, wd, bd, lg, lbias, bblk, tt):
    b, t, _ = proj3.shape
    c0 = COL_CONV // D_BR
    col = lambda c: pl.BlockSpec((bblk, tt, D_BR), lambda i, j: (i, j, c0 + c))
    full2 = lambda shape: pl.BlockSpec(shape, lambda i, j: (0, 0))
    full3 = lambda shape: pl.BlockSpec(shape, lambda i, j: (0, 0, 0))
    rows8 = lambda w: jnp.broadcast_to(w[:, None, :], (w.shape[0], SUBLANES, D_BR))
    shapes = [(bblk, CONV_C - 1, D_BR), (bblk, CONV_D - 1, D_BR)]
    outc =pl.BlockSpec((bblk, tt, D_BR), lambda i, j: (i, j, 0))
    aliases = {}
    if buf_prev is None:
        buf_prev = [jnp.zeros((1,), F32)] * 2
    else:
        aliases = {12: 2, 13: 3}
    any_spec = pl.BlockSpec(memory_space=pl.ANY)
    return pl.pallas_call(
        _conv_kernel,
        grid=(b // bblk, t // tt),
        in_specs=[col(0), col(1), col(2), col(3), col(4)] + [_layer_block(s, l_in, 2) for s in shapes]
                 + [full3((CONV_C, SUBLANES, D_BR)), full3((CONV_D, SUBLANES, D_BR)), full2((1, D_BR)),
                    full2((1, D_BR)), full2((1, D_BR)), any_spec, any_spec],
        out_specs=[outc, outc] + [_layer_block(s, l, 2) for s in shapes],
        out_shape=[jax.ShapeDtypeStruct((b, t, D_BR), BF16), jax.ShapeDtypeStruct((b, t, D_BR), BF16)]
                  + [jax.ShapeDtypeStruct((depth, b) + s[1:], F32) for s in shapes],
        scratch_shapes=[pltpu.VMEM((bblk, PAD_C + tt, D_BR), F32), pltpu.VMEM((bblk, PAD_D + tt, D_BR), F32),
                        pltpu.VMEM((CONV_C - 1, tt, D_BR), F32), pltpu.VMEM((SUBLANES, PAD_D + tt, D_BR), F32)],
        input_output_aliases=aliases,
        compiler_params=_cparams(("parallel", "arbitrary")),
        name="conv",
    )(proj3, proj3, proj3, proj3, proj3, *buf_in, rows8(wc), rows8(wd), bd, lg, lbias, *buf_prev)


FF_TILE = 1024
MIX_VMEM_LIMIT = 60 * 1024 * 1024


def _mix_kernel(x_ref, oa_ref, ob_ref, oc_ref, od_ref, gt_ref, wb_ref, wo_ref, nm_ref, wu_ref, wdn_ref,
                nn_ref, *out_refs, last):
    merged = None
    for n, br in enumerate((oa_ref, ob_ref, oc_ref, od_ref)):
        gate = _sigmoid(gt_ref[:, n * D_MODEL:(n + 1) * D_MODEL])
        term = gate * jnp.dot(br[...], wb_ref[n], preferred_element_type=F32)
        merged = term if merged is None else merged + term
    x = x_ref[...] + jnp.dot(merged.astype(BF16), wo_ref[...], preferred_element_type=F32)
    hm = _rms(x, nm_ref[...]).astype(BF16)
    for c in range(D_FF // FF_TILE):
        cs = slice(c * FF_TILE, (c + 1) * FF_TILE)
        up = jnp.maximum(jnp.dot(hm, wu_ref[:, cs], preferred_element_type=F32), 0.0)
        x = x + jnp.dot((up * up).astype(BF16), wdn_ref[cs, :], preferred_element_type=F32)
    if last:
        out_refs[0][...] = _rms(x, nn_ref[...])
    else:
        out_refs[0][...] = x
        out_refs[1][...] = _rms(x, nn_ref[...]).astype(BF16)


def _mix(x, oa, ob, oc, od, proj, wb, wo, nm, wu, wdn, nn, l, last):
    n = x.shape[0]
    tm = min(n, ROW_TILE)
    row = lambda w: pl.BlockSpec((tm, w), lambda i: (i, 0))
    once = pl.Buffered(1)
    full2 = lambda shape: pl.BlockSpec(shape, lambda i: (0, 0))
    layer3 = lambda shape: pl.BlockSpec((None,) + shape, lambda i: (l, 0, 0), pipeline_mode=once)
    out_specs = [row(D_MODEL)] if last else [row(D_MODEL), row(D_MODEL)]
    out_shape = [jax.ShapeDtypeStruct((n, D_MODEL), F32)]
    if not last:
        out_shape.append(jax.ShapeDtypeStruct((n, D_MODEL), BF16))
    return pl.pallas_call(
        functools.partial(_mix_kernel, last=last),
        grid=(n // tm,),
        in_specs=[row(D_MODEL), row(D_BR), row(D_BR), row(D_BR), row(D_BR),
                  pl.BlockSpec((tm, N_GATE), lambda i: (i, COL_GATE // N_GATE)),
                  pl.BlockSpec((None, N_BRANCH, D_BR, D_MODEL), lambda i: (l, 0, 0, 0), pipeline_mode=once),
                  layer3((D_MODEL, D_MODEL)), full2((1, D_MODEL)), layer3((D_MODEL, D_FF)),
                  layer3((D_FF, D_MODEL)), full2((1, D_MODEL))],
        out_specs=out_specs,
        out_shape=out_shape,
        compiler_params=_cparams(("parallel",), MIX_VMEM_LIMIT),
        name="mix",
    )(x, oa, ob, oc, od, proj, wb, wo, nm, wu, wdn, nn)


def _layer(x, h, b, t, st_in, l_in, st_prev, l, depth, lw, wts, last):
    (lb, n_hgrn, bias_c, bias_r, n_mlstm, wc, wd, bd, lg, lbias, n_mlp, n_next) = lw
    w_in, wb, wo, wu, wdn = wts
    L = math.gcd(t, CHUNK)
    G = min(b, SEQ_GROUP)
    proj = _proj(h, w_in, l)
    proj3 = proj.reshape(b, t, D_PROJ)
    small = proj3[:, :, COL_SMALL:COL_SMALL + 2 * N_HEAD]
    small_row = jnp.swapaxes(small.reshape(b, t // L, L, 2 * N_HEAD), 2, 3)

    prev = (None,) * 6 if st_prev is None else st_prev
    g_h, tt_h = (2, 512) if t >= 512 else (G, t)
    oa, s_hgrn = _hgrn(proj3, lb, n_hgrn, st_in[0], l_in, prev[0], l, depth, L, g_h, tt_h)
    tt_m = min(t, 128)
    ob, s_c, s_n, s_m = _mlstm(proj3, small_row, bias_c, bias_r, n_mlstm, st_in[1:4], l_in,
                               None if st_prev is None else prev[1:4], l, depth, L, G, tt_m)
    g_c, tt_c = (1, 512) if t >= 512 else (G, t)
    oc, od, buf_c, buf_d = _conv(proj3, st_in[4:6], l_in, None if st_prev is None else prev[4:6], l, depth,
                                 wc, wd, bd, lg, lbias, g_c, tt_c)
    flat = lambda a: a.reshape(b * t, D_BR)
    outs = _mix(x, flat(oa), flat(ob), flat(oc), flat(od), proj, wb, wo, n_mlp, wu, wdn, n_next, l, last)
    return outs, (s_hgrn, s_c, s_n, s_m, buf_c, buf_d)


def kernel(x_prompt, x_sample, state_hgrn, state_mlstm_c, state_mlstm_n, state_mlstm_m, state_conv_short, state_conv_conformer, w_in, w_branch, w_out, lb_logits, norm_hgrn, mlstm_i_bias, mlstm_f_bias, norm_mlstm, conv_short_w, conv_conformer_w, conv_conformer_b, ln_conformer_g, ln_conformer_b, norm_mix, norm_mlp, w_up, w_down, norm_final):
    depth = w_in.shape[0]
    bp, tp, _ = x_prompt.shape
    bs, ts, _ = x_sample.shape
    lower_bounds = _lower_bounds(lb_logits)

    wts = (_permute_w_in(w_in),) + tuple(a.astype(BF16) for a in (w_branch, w_out, w_up, w_down))
    gate_bias = jnp.concatenate([mlstm_i_bias, mlstm_f_bias], axis=1).astype(F32)
    bias_c = jnp.pad(gate_bias, ((0, 0), (0, N_SMALL - 2 * N_HEAD)))[:, None, :]
    bias_r = gate_bias[:, :, None]
    row = lambda a, l: a[l][None, :].astype(F32)

    zero_st = (jnp.zeros((1, bp, N_HEAD, D_HEAD, D_HEAD), F32), jnp.zeros((1, bp, N_HEAD, D_HEAD, D_HEAD), F32),
               jnp.zeros((1, bp, N_HEAD, 1, D_HEAD), F32), jnp.zeros((1, bp, 1, N_SMALL), F32),
               jnp.zeros((1, bp, CONV_C - 1, D_BR), F32), jnp.zeros((1, bp, CONV_D - 1, D_BR), F32))
    m_lanes = jnp.pad(state_mlstm_m.astype(F32), ((0, 0), (0, 0), (0, N_SMALL - N_HEAD)))[:, :, None, :]
    past_st = (state_hgrn, state_mlstm_c, state_mlstm_n[:, :, :, None, :], m_lanes,
               state_conv_short, state_conv_conformer)

    xp = x_prompt.reshape(bp * tp, D_MODEL)
    xs = x_sample.reshape(bs * ts, D_MODEL)
    hp = _norm(xp, row(norm_mix, 0))
    hs = _norm(xs, row(norm_mix, 0))
    st_p = st_s = None
    for l in range(depth):
        last = l == depth - 1
        n_next = norm_final[None, :].astype(F32) if last else row(norm_mix, l + 1)
        lw = (lower_bounds[l][None, :], row(norm_hgrn, l), bias_c[l], bias_r[l], row(norm_mlstm, l),
              conv_short_w[l], conv_conformer_w[l], row(conv_conformer_b, l), row(ln_conformer_g, l),
              row(ln_conformer_b, l), row(norm_mlp, l), n_next)
        outs_p, st_p = _layer(xp, hp, bp, tp, zero_st, 0, st_p, l, depth, lw, wts, last)
        outs_s, st_s = _layer(xs, hs, bs, ts, past_st, l, st_s, l, depth, lw, wts, last)
        if last:
            xp, xs = outs_p[0], outs_s[0]
        else:
            (xp, hp), (xs, hs) = outs_p, outs_s

    def finish(x, b, t, st):
        s_hgrn, s_c, s_n, s_m, buf_c, buf_d = st
        return (x.reshape(b, t, D_MODEL), s_hgrn, s_c, s_n[:, :, :, 0, :], s_m[:, :, 0, :N_HEAD], buf_c, buf_d)

    fp = finish(xp, bp, tp, st_p)
    fs = finish(xs, bs, ts, st_s)
    outs = []
    for a, c in zip(fp, fs):
        outs += [a, c]
    return tuple(outs)
```

```python
import functools
import math

import jax
import jax.numpy as jnp
from jax import lax
from jax.experimental import pallas as pl
from jax.experimental.pallas import tpu as pltpu

F32 = jnp.float32
BF16 = jnp.bfloat16

D_MODEL = 1024
N_HEAD = 4
D_HEAD = 128
D_BR = 512
N_BRANCH = 4
CONV_C = 3
CONV_D = 31
D_FF = 4 * D_MODEL
CHUNK = 64
EPS = 1e-6
SUBLANES = 8
LANES = 128

N_MAIN = 8 * D_BR
N_GATE = N_BRANCH * D_MODEL
N_CONV = 5 * D_BR
N_SMALL = LANES
D_PROJ = N_MAIN + N_GATE + N_CONV + N_SMALL
COL_GATE = N_MAIN
COL_CONV = N_MAIN + N_GATE
COL_SMALL = N_MAIN + N_GATE + N_CONV
PROJ_TN = D_PROJ // 5

VMEM_LIMIT = 56 * 1024 * 1024
ROW_TILE = 512
SEQ_GROUP = 8


def _cparams(sem, vmem_limit=VMEM_LIMIT):
    return pltpu.CompilerParams(dimension_semantics=sem, vmem_limit_bytes=vmem_limit)


def _dot(a, b):
    return jnp.dot(a.astype(BF16), b.astype(BF16), preferred_element_type=F32)


def _dot_nt(a, b):
    return lax.dot_general(a.astype(BF16), b.astype(BF16), (((1,), (1,)), ((), ())),
                           preferred_element_type=F32)


def _dot_tn(a, b):
    return lax.dot_general(a.astype(BF16), b.astype(BF16), (((0,), (0,)), ((), ())),
                           preferred_element_type=F32)


def _split3(x):
    hi = x.astype(BF16)
    r = x - hi.astype(F32)
    mid = r.astype(BF16)
    lo = (r - mid.astype(F32)).astype(BF16)
    return hi, mid, lo


def _sigmoid(x):
    return jax.nn.sigmoid(x)


def _log_sigmoid(x):
    return jnp.minimum(x, 0.0) - jnp.log1p(jnp.exp(-jnp.abs(x)))


def _rms(x, g):
    return x * lax.rsqrt(jnp.mean(x * x, axis=-1, keepdims=True) + EPS) * g


def _layer_block(shape, l, grid_rank):
    zeros = (0,) * (len(shape) - 1)
    if grid_rank == 2:
        return pl.BlockSpec((None,) + shape, lambda i, j: (l, i) + zeros)
    return pl.BlockSpec((None,) + shape, lambda i: (l, i) + zeros)


def _lb_kernel(x_ref, o_ref):
    x = x_ref[...]
    depth = x.shape[0]
    e = jnp.exp(x - jnp.max(x, axis=0, keepdims=True))
    s = e / jnp.sum(e, axis=0, keepdims=True)
    acc = jnp.zeros_like(s[0:1])
    for i in range(depth):
        acc = acc + s[i:i + 1]
        o_ref[i:i + 1, :] = acc - s[0:1]


def _lower_bounds(lb_logits):
    return pl.pallas_call(
        _lb_kernel, out_shape=jax.ShapeDtypeStruct(lb_logits.shape, F32), name="lower_bounds",
    )(lb_logits.astype(F32))


def _norm_kernel(x_ref, g_ref, o_ref):
    o_ref[...] = _rms(x_ref[...], g_ref[...]).astype(o_ref.dtype)


def _norm(x, g):
    n = x.shape[0]
    tm = min(n, ROW_TILE)
    return pl.pallas_call(
        _norm_kernel,
        grid=(n // tm,),
        in_specs=[pl.BlockSpec((tm, D_MODEL), lambda i: (i, 0)), pl.BlockSpec((1, D_MODEL), lambda i: (0, 0))],
        out_specs=pl.BlockSpec((tm, D_MODEL), lambda i: (i, 0)),
        out_shape=jax.ShapeDtypeStruct((n, D_MODEL), BF16),
        compiler_params=_cparams(("parallel",)),
        name="norm",
    )(x, g)


PERM_LANES = 128


def _permute_kernel(w_ref, o_ref):
    o_small = N_MAIN
    o_conv = o_small + 2 * N_HEAD
    o_gate = o_conv + N_CONV
    o_ref[:N_MAIN, :] = w_ref[:N_MAIN, :].astype(o_ref.dtype)
    o_ref[COL_GATE:COL_GATE + N_GATE, :] = w_ref[o_gate:o_gate + N_GATE, :].astype(o_ref.dtype)
    o_ref[COL_CONV:COL_CONV + N_CONV, :] = w_ref[o_conv:o_conv + N_CONV, :].astype(o_ref.dtype)
    pad = jnp.zeros((N_SMALL - 2 * N_HEAD, o_ref.shape[1]), F32)
    o_ref[COL_SMALL:, :] = jnp.concatenate([w_ref[o_small:o_conv, :], pad], axis=0).astype(o_ref.dtype)


def _permute_w_in(w_in_t):
    depth, d_in, d = w_in_t.shape
    return pl.pallas_call(
        _permute_kernel,
        grid=(depth, d // PERM_LANES),
        in_specs=[pl.BlockSpec((None, d_in, PERM_LANES), lambda l, i: (l, 0, i))],
        out_specs=pl.BlockSpec((None, D_PROJ, PERM_LANES), lambda l, i: (l, 0, i)),
        out_shape=jax.ShapeDtypeStruct((depth, D_PROJ, d), BF16),
        compiler_params=_cparams(("parallel", "parallel")),
        name="permute_w_in",
    )(w_in_t)


def _proj_kernel(h_ref, w_ref, o_ref):
    o_ref[...] = lax.dot_general(h_ref[...], w_ref[...], (((1,), (1,)), ((), ())), preferred_element_type=F32)


def _proj(h, w_t, l):
    n = h.shape[0]
    tm = min(n, 2 * ROW_TILE)
    return pl.pallas_call(
        _proj_kernel,
        grid=(D_PROJ // PROJ_TN, n // tm),
        in_specs=[pl.BlockSpec((tm, D_MODEL), lambda j, i: (i, 0)),
                  pl.BlockSpec((None, PROJ_TN, D_MODEL), lambda j, i: (l, j, 0))],
        out_specs=pl.BlockSpec((tm, PROJ_TN), lambda j, i: (i, j)),
        out_shape=jax.ShapeDtypeStruct((n, D_PROJ), F32),
        compiler_params=_cparams(("parallel", "parallel")),
        name="proj",
    )(h, w_t)


def _cumsum_rows(x):
    n = x.shape[0]
    row = lax.broadcasted_iota(jnp.int32, x.shape, 0)
    s = 1
    while s < n:
        x = x + jnp.where(row >= s, pltpu.roll(x, s, 0), 0.0)
        s *= 2
    return x


def _group_row(x, group, j):
    n, w = x.shape
    x3 = x.reshape(n // group, group, w)
    return jnp.broadcast_to(x3[:, j:j + 1, :], x3.shape).reshape(n, w)


def _boundary_row(x, c, row):
    if 2 * c >= SUBLANES:
        return _group_row(x, 2 * c, c - 1)
    if c == 2:
        return jnp.where(row % SUBLANES < 4, _group_row(x, SUBLANES, 1), _group_row(x, SUBLANES, 5))
    return jnp.where(row % 2 == 1, pltpu.roll(x, 1, 0), x)


def _hgrn_kernel(qa_ref, fa_ref, ia_ref, ga_ref, lb_ref, nw_ref, s0_ref, prev_ref, o_ref, s_ref, *, L):
    del prev_ref
    G, tt, _ = qa_ref.shape
    nchunk = tt // L
    levels = [c for c in (32, 16, 8, 4, 2, 1) if 2 * c <= L]

    @pl.when(pl.program_id(1) == 0)
    def _():
        s_ref[...] = s0_ref[...]

    lb = lb_ref[...]
    nw = nw_ref[...]
    row = lax.broadcasted_iota(jnp.int32, (L, D_BR), 0)
    row_l = lax.broadcasted_iota(jnp.int32, (L, L), 0)
    col_l = lax.broadcasted_iota(jnp.int32, (L, L), 1)
    heads = [slice(h * D_HEAD, (h + 1) * D_HEAD) for h in range(N_HEAD)]
    level = jnp.where(row_l == col_l, 0.0, -1.0)
    for c in levels:
        pair = ((row_l // (2 * c)) == (col_l // (2 * c))) & (row_l % (2 * c) >= c) & (col_l % (2 * c) < c)
        level = jnp.where(pair, float(c), level)

    def body(ci, carry):
        rows = pl.ds(pl.multiple_of(ci * L, L), L)
        seqs = range(G)
        k, q, gc, out, scores = {}, {}, {}, {}, {}
        for bi in seqs:
            f = lb + (1.0 - lb) * _sigmoid(fa_ref[bi, rows, :])
            k[bi] = 1.0 - f
            qa = qa_ref[bi, rows, :]
            q[bi] = qa * _sigmoid(qa)
            gc[bi] = _cumsum_rows(jnp.log(f))
        for bi in seqs:
            qg = q[bi] * jnp.exp(gc[bi])
            for h, hs in enumerate(heads):
                out[bi, h] = _dot(qg[:, hs], s_ref[bi, h])
                scores[bi, h] = jnp.where(level == 0.0, _dot_nt(q[bi][:, hs], k[bi][:, hs]), 0.0)
        for c in levels:
            for bi in seqs:
                diff = gc[bi] - _boundary_row(gc[bi], c, row)
                qe = q[bi] * jnp.exp(diff)
                ke = k[bi] * jnp.exp(-diff)
                for h, hs in enumerate(heads):
                    scores[bi, h] = jnp.where(level == float(c), _dot_nt(qe[:, hs], ke[:, hs]), scores[bi, h])
        for bi in seqs:
            v = ia_ref[bi, rows, :]
            g_last = gc[bi][L - 1:L, :]
            k_dec = k[bi] * jnp.exp(g_last - gc[bi])
            decay = jnp.exp(g_last)
            ga = ga_ref[bi, rows, :]
            gate = ga * _sigmoid(ga)
            for h, hs in enumerate(heads):
                o = out[bi, h] + _dot(scores[bi, h], v[:, hs])
                decay_col = jnp.broadcast_to(decay[:, hs], (D_HEAD, D_HEAD)).T
                s_ref[bi, h] = decay_col * s_ref[bi, h] + _dot_tn(k_dec[:, hs], v[:, hs])
                y = o * lax.rsqrt(jnp.mean(o * o, axis=-1, keepdims=True) + EPS) * nw[:, hs]
                o_ref[bi, rows, hs] = (y * gate[:, hs]).astype(o_ref.dtype)
        return carry

    lax.fori_loop(0, nchunk, body, 0)


def _hgrn(proj3, lb, nw, s_in, l_in, s_prev, l, depth, L, G, tt):
    b, t, _ = proj3.shape
    col = lambda c: pl.BlockSpec((G, tt, D_BR), lambda i, j: (i, j, c))
    vec = pl.BlockSpec((1, D_BR), lambda i, j: (0, 0))
    st = (G, N_HEAD, D_HEAD, D_HEAD)
    aliases = {}
    if s_prev is None:
        s_prev = jnp.zeros((1,), F32)
    else:
        aliases = {7: 1}
    return pl.pallas_call(
        functools.partial(_hgrn_kernel, L=L),
        grid=(b // G, t // tt),
        in_specs=[col(0), col(1), col(2), col(3), vec, vec, _layer_block(st, l_in, 2),
                  pl.BlockSpec(memory_space=pl.ANY)],
        out_specs=[pl.BlockSpec((G, tt, D_BR), lambda i, j: (i, j, 0)), _layer_block(st, l, 2)],
        out_shape=[jax.ShapeDtypeStruct((b, t, D_BR), BF16),
                   jax.ShapeDtypeStruct((depth, b) + st[1:], F32)],
        input_output_aliases=aliases,
        compiler_params=_cparams(("parallel", "arbitrary")),
        name="hgrn",
    )(proj3, proj3, proj3, proj3, lb, nw, s_in, s_prev)


def _mlstm_kernel(q_ref, k_ref, v_ref, og_ref, sc_ref, sr_ref, bc_ref, br_ref, nw_ref,
                  c0_ref, n0_ref, m0_ref, cp_ref, np_ref, mp_ref, o_ref, c_ref, n_ref, m_ref, *, L):
    del cp_ref, np_ref, mp_ref
    G, tt, _ = q_ref.shape
    nchunk = tt // L

    @pl.when(pl.program_id(1) == 0)
    def _():
        c_ref[...] = c0_ref[...]
        n_ref[...] = n0_ref[...]
        m_ref[...] = m0_ref[...]

    nw = nw_ref[...]
    bias_c = bc_ref[...]
    bias_r = br_ref[...]
    row_l = lax.broadcasted_iota(jnp.int32, (L, L), 0)
    col_l = lax.broadcasted_iota(jnp.int32, (L, L), 1)
    causal = row_l >= col_l
    tri = jnp.where(causal, 1.0, 0.0).astype(BF16)
    tri_t = jnp.where(row_l <= col_l, 1.0, 0.0).astype(BF16)
    row_c = lax.broadcasted_iota(jnp.int32, (L, N_SMALL), 0)
    ones = jnp.ones((L, D_HEAD), F32)
    scale = D_HEAD ** -0.5
    heads = [slice(h * D_HEAD, (h + 1) * D_HEAD) for h in range(N_HEAD)]
    group = G

    def gates(bi, ci, rows):
        pre_c = sc_ref[bi, rows, :] + bias_c
        pre_r = sr_ref[bi, ci] + bias_r
        hi, mid, lo = _split3(_log_sigmoid(pre_c))
        b_c = (jnp.dot(tri, hi, preferred_element_type=F32) + jnp.dot(tri, mid, preferred_element_type=F32)
               + jnp.dot(tri, lo, preferred_element_type=F32))
        hi, mid, lo = _split3(_log_sigmoid(pre_r))
        b_r = (jnp.dot(hi, tri_t, preferred_element_type=F32) + jnp.dot(mid, tri_t, preferred_element_type=F32)
               + jnp.dot(lo, tri_t, preferred_element_type=F32))
        b = pltpu.roll(b_c, N_SMALL - N_HEAD, 1)
        a = pre_c - b
        cm = a
        s = 1
        while s < L:
            cm = jnp.maximum(cm, jnp.where(row_c >= s, pltpu.roll(cm, s, 0), -jnp.inf))
            s *= 2
        m_old = m_ref[bi]
        big_m = jnp.maximum(m_old, cm)
        m_last = big_m[L - 1:L, :]
        m_ref[bi] = b[L - 1:L, :] + m_last
        return dict(a_r=pre_r[0:N_HEAD, :] - b_r[N_HEAD:2 * N_HEAD, :], big_m=big_m, m_old=m_old,
                    w_inter=jnp.exp(m_old - big_m), n_exp=jnp.exp(-(b + big_m)),
                    w_state=jnp.exp(a - m_last), carry=jnp.exp(m_old - m_last))

    def body(ci, carry):
        rows = pl.ds(pl.multiple_of(ci * L, L), L)
        for g0 in range(0, G, group):
            seqs = range(g0, g0 + group)
            sh = {bi: gates(bi, ci, rows) for bi in seqs}
            probs = [(bi, h) for bi in seqs for h in range(N_HEAD)]
            qk, wi_b, intra, qc, qn = {}, {}, {}, {}, {}
            for p in probs:
                bi, h = p
                g = sh[bi]
                m_b = jnp.broadcast_to(g["big_m"][:, h:h + 1], (L, D_HEAD))
                wi_b[p] = jnp.exp(g["m_old"][:, h:h + 1] - m_b)
                w = jnp.where(causal, jnp.exp(g["a_r"][h:h + 1, :] - m_b[:, :L]), 0.0)
                q = q_ref[bi, rows, heads[h]] * scale
                qk[p] = _dot_nt(q, k_ref[bi, rows, heads[h]]) * w
            for p in probs:
                bi, h = p
                q = q_ref[bi, rows, heads[h]] * scale
                intra[p] = _dot(qk[p], jnp.concatenate([v_ref[bi, rows, heads[h]], ones], axis=1))
                qc[p] = _dot(q, c_ref[bi, h])
                qn[p] = _dot_nt(q, jnp.broadcast_to(n_ref[bi, h], (D_HEAD, D_HEAD)))
            for p in probs:
                bi, h = p
                g = sh[bi]
                num = intra[p][:, :D_HEAD] + wi_b[p] * qc[p]
                den = intra[p][:, D_HEAD + h:D_HEAD + h + 1] + g["w_inter"][:, h:h + 1] * qn[p][:, h:h + 1]
                r = 1.0 / jnp.maximum(jnp.abs(den), g["n_exp"][:, h:h + 1])
                t = r * lax.rsqrt(r * r * jnp.mean(num * num, axis=-1, keepdims=True) + EPS)
                gate = _sigmoid(og_ref[bi, rows, heads[h]])
                o_ref[bi, rows, heads[h]] = (num * t * nw[:, heads[h]] * gate).astype(o_ref.dtype)
            for p in probs:
                bi, h = p
                g = sh[bi]
                k = k_ref[bi, rows, heads[h]]
                kw = g["w_state"][:, h:h + 1] * k
                cd = g["carry"][:, h:h + 1]
                c_ref[bi, h] = cd * c_ref[bi, h] + _dot_tn(kw, v_ref[bi, rows, heads[h]])
                n_ref[bi, h] = cd * n_ref[bi, h] + jnp.sum(kw, axis=0, keepdims=True)
        return carry

    lax.fori_loop(0, nchunk, body, 0)


def _mlstm(proj3, small_row, bias_c, bias_r, nw, st_in, l_in, st_prev, l, depth, L, G, tt):
    b, t, _ = proj3.shape
    col = lambda c: pl.BlockSpec((G, tt, D_BR), lambda i, j: (i, j, c))
    full2 = lambda shape: pl.BlockSpec(shape, lambda i, j: (0, 0))
    shapes = [(G, N_HEAD, D_HEAD, D_HEAD), (G, N_HEAD, 1, D_HEAD), (G, 1, N_SMALL)]
    aliases = {}
    if st_prev is None:
        st_prev = [jnp.zeros((1,), F32)] * 3
    else:
        aliases = {12: 1, 13: 2, 14: 3}
    any_spec = pl.BlockSpec(memory_space=pl.ANY)
    return pl.pallas_call(
        functools.partial(_mlstm_kernel, L=L),
        grid=(b // G, t // tt),
        in_specs=[col(4), col(5), col(6), col(7),
                  pl.BlockSpec((G, tt, N_SMALL), lambda i, j: (i, j, COL_SMALL // N_SMALL)),
                  pl.BlockSpec((G, tt // L, 2 * N_HEAD, L), lambda i, j: (i, j, 0, 0)),
                  full2((1, N_SMALL)), full2((2 * N_HEAD, 1)), full2((1, D_BR))]
                 + [_layer_block(s, l_in, 2) for s in shapes] + [any_spec] * 3,
        out_specs=[pl.BlockSpec((G, tt, D_BR), lambda i, j: (i, j, 0))] + [_layer_block(s, l, 2) for s in shapes],
        out_shape=[jax.ShapeDtypeStruct((b, t, D_BR), BF16)]
                  + [jax.ShapeDtypeStruct((depth, b) + s[1:], F32) for s in shapes],
        input_output_aliases=aliases,
        compiler_params=_cparams(("parallel", "arbitrary")),
        name="mlstm",
    )(proj3, proj3, proj3, proj3, proj3, small_row, bias_c, bias_r, nw, *st_in, *st_prev)


PAD_C = 8
PAD_D = 32
CONV_ROWS = 32


def _conv_kernel(bc_ref, cc_ref, xc_ref, ad_ref, gd_ref, bufc_ref, bufd_ref, wc_ref, wd_ref, bd_ref,
                 lg_ref, lbias_ref, pc_ref, pd_ref, oc_ref, od_ref, nbc_ref, nbd_ref, extc, extd, shc, shd):
    del pc_ref, pd_ref
    bblk, tt, _ = bc_ref.shape
    hc, hd = CONV_C - 1, CONV_D - 1
    rt = min(tt, CONV_ROWS)
    offs_c = [PAD_C - hc + j for j in range(CONV_C)]
    offs_d = [PAD_D - hd + j for j in range(CONV_D)]
    unaligned_c = [o for o in offs_c if o % SUBLANES]
    span_d = tt + PAD_D - SUBLANES

    @pl.when(pl.program_id(1) == 0)
    def _():
        extc[:, PAD_C - hc:PAD_C, :] = bufc_ref[...]
        extd[:, PAD_D - hd:PAD_D, :] = bufd_ref[...]

    extc[:, PAD_C:PAD_C + tt, :] = cc_ref[...] * xc_ref[...]
    extd[:, PAD_D:PAD_D + tt, :] = ad_ref[...] * _sigmoid(gd_ref[...])

    def tile_rows(w8):
        return w8 if rt == SUBLANES else jnp.concatenate([w8] * (rt // SUBLANES), axis=0)

    def seq_body(bi, carry):
        for res in range(1, SUBLANES):
            shd[res, 0:span_d, :] = extd[bi, res:res + span_d, :]
        for i, o in enumerate(unaligned_c):
            shc[i, 0:tt, :] = extc[bi, o:o + tt, :]

        def tile_body(ti, c2):
            r0 = pl.multiple_of(ti * rt, rt)
            acc = jnp.zeros((rt, D_BR), F32)
            for j, o in enumerate(offs_c):
                if o % SUBLANES:
                    x = shc[unaligned_c.index(o), pl.ds(r0, rt), :]
                else:
                    x = extc[bi, pl.ds(pl.multiple_of(r0 + o, SUBLANES), rt), :]
                acc = acc + tile_rows(wc_ref[j]) * x
            oc_ref[bi, pl.ds(r0, rt), :] = (bc_ref[bi, pl.ds(r0, rt), :] * acc).astype(oc_ref.dtype)
            acc = jnp.zeros((rt, D_BR), F32)
            for j, o in enumerate(offs_d):
                res = o % SUBLANES
                rows = pl.ds(pl.multiple_of(r0 + (o - res), SUBLANES), rt)
                x = shd[res, rows, :] if res else extd[bi, rows, :]
                acc = acc + tile_rows(wd_ref[j]) * x
            y = acc + bd_ref[...]
            yc = y - jnp.mean(y, axis=-1, keepdims=True)
            z = yc * lax.rsqrt(jnp.mean(yc * yc, axis=-1, keepdims=True) + EPS) * lg_ref[...] + lbias_ref[...]
            od_ref[bi, pl.ds(r0, rt), :] = (z * _sigmoid(z)).astype(od_ref.dtype)
            return c2

        lax.fori_loop(0, tt // rt, tile_body, 0, unroll=4 if (tt // rt) % 4 == 0 else 1)
        return carry

    lax.fori_loop(0, bblk, seq_body, 0)

    new_c = extc[:, PAD_C + tt - hc:PAD_C + tt, :]
    new_d = extd[:, PAD_D + tt - hd:PAD_D + tt, :]
    nbc_ref[...] = new_c
    nbd_ref[...] = new_d
    extc[:, PAD_C - hc:PAD_C, :] = new_c
    extd[:, PAD_D - hd:PAD_D, :] = new_d


def _conv(proj3, buf_in, l_in, buf_prev, l, depth, wc, wd, bd, lg, lbias, bblk, tt):
    b, t, _ = proj3.shape
    c0 = COL_CONV // D_BR
    col = lambda c: pl.BlockSpec((bblk, tt, D_BR), lambda i, j: (i, j, c0 + c))
    full2 = lambda shape: pl.BlockSpec(shape, lambda i, j: (0, 0))
    full3 = lambda shape: pl.BlockSpec(shape, lambda i, j: (0, 0, 0))
    rows8 = lambda w: jnp.broadcast_to(w[:, None, :], (w.shape[0], SUBLANES, D_BR))
    shapes = [(bblk, CONV_C - 1, D_BR), (bblk, CONV_D - 1, D_BR)]
    outc =pl.BlockSpec((bblk, tt, D_BR), lambda i, j: (i, j, 0))
    aliases = {}
    if buf_prev is None:
        buf_prev = [jnp.zeros((1,), F32)] * 2
    else:
        aliases = {12: 2, 13: 3}
    any_spec = pl.BlockSpec(memory_space=pl.ANY)
    return pl.pallas_call(
        _conv_kernel,
        grid=(b // bblk, t // tt),
        in_specs=[col(0), col(1), col(2), col(3), col(4)] + [_layer_block(s, l_in, 2) for s in shapes]
                 + [full3((CONV_C, SUBLANES, D_BR)), full3((CONV_D, SUBLANES, D_BR)), full2((1, D_BR)),
                    full2((1, D_BR)), full2((1, D_BR)), any_spec, any_spec],
        out_specs=[outc, outc] + [_layer_block(s, l, 2) for s in shapes],
        out_shape=[jax.ShapeDtypeStruct((b, t, D_BR), BF16), jax.ShapeDtypeStruct((b, t, D_BR), BF16)]
                  + [jax.ShapeDtypeStruct((depth, b) + s[1:], F32) for s in shapes],
        scratch_shapes=[pltpu.VMEM((bblk, PAD_C + tt, D_BR), F32), pltpu.VMEM((bblk, PAD_D + tt, D_BR), F32),
                        pltpu.VMEM((CONV_C - 1, tt, D_BR), F32), pltpu.VMEM((SUBLANES, PAD_D + tt, D_BR), F32)],
        input_output_aliases=aliases,
        compiler_params=_cparams(("parallel", "arbitrary")),
        name="conv",
    )(proj3, proj3, proj3, proj3, proj3, *buf_in, rows8(wc), rows8(wd), bd, lg, lbias, *buf_prev)


FF_TILE = 1024
MIX_VMEM_LIMIT = 60 * 1024 * 1024


def _mix_kernel(x_ref, oa_ref, ob_ref, oc_ref, od_ref, gt_ref, wb_ref, wo_ref, nm_ref, wu_ref, wdn_ref,
                nn_ref, *out_refs, last):
    merged = None
    for n, br in enumerate((oa_ref, ob_ref, oc_ref, od_ref)):
        gate = _sigmoid(gt_ref[:, n * D_MODEL:(n + 1) * D_MODEL])
        term = gate * jnp.dot(br[...], wb_ref[n], preferred_element_type=F32)
        merged = term if merged is None else merged + term
    x = x_ref[...] + jnp.dot(merged.astype(BF16), wo_ref[...], preferred_element_type=F32)
    hm = _rms(x, nm_ref[...]).astype(BF16)
    for c in range(D_FF // FF_TILE):
        cs = slice(c * FF_TILE, (c + 1) * FF_TILE)
        up = jnp.maximum(jnp.dot(hm, wu_ref[:, cs], preferred_element_type=F32), 0.0)
        x = x + jnp.dot((up * up).astype(BF16), wdn_ref[cs, :], preferred_element_type=F32)
    if last:
        out_refs[0][...] = _rms(x, nn_ref[...])
    else:
        out_refs[0][...] = x
        out_refs[1][...] = _rms(x, nn_ref[...]).astype(BF16)


def _mix(x, oa, ob, oc, od, proj, wb, wo, nm, wu, wdn, nn, l, last):
    n = x.shape[0]
    tm = min(n, ROW_TILE)
    row = lambda w: pl.BlockSpec((tm, w), lambda i: (i, 0))
    once = pl.Buffered(1)
    full2 = lambda shape: pl.BlockSpec(shape, lambda i: (0, 0))
    layer3 = lambda shape: pl.BlockSpec((None,) + shape, lambda i: (l, 0, 0), pipeline_mode=once)
    out_specs = [row(D_MODEL)] if last else [row(D_MODEL), row(D_MODEL)]
    out_shape = [jax.ShapeDtypeStruct((n, D_MODEL), F32)]
    if not last:
        out_shape.append(jax.ShapeDtypeStruct((n, D_MODEL), BF16))
    return pl.pallas_call(
        functools.partial(_mix_kernel, last=last),
        grid=(n // tm,),
        in_specs=[row(D_MODEL), row(D_BR), row(D_BR), row(D_BR), row(D_BR),
                  pl.BlockSpec((tm, N_GATE), lambda i: (i, COL_GATE // N_GATE)),
                  pl.BlockSpec((None, N_BRANCH, D_BR, D_MODEL), lambda i: (l, 0, 0, 0), pipeline_mode=once),
                  layer3((D_MODEL, D_MODEL)), full2((1, D_MODEL)), layer3((D_MODEL, D_FF)),
                  layer3((D_FF, D_MODEL)), full2((1, D_MODEL))],
        out_specs=out_specs,
        out_shape=out_shape,
        compiler_params=_cparams(("parallel",), MIX_VMEM_LIMIT),
        name="mix",
    )(x, oa, ob, oc, od, proj, wb, wo, nm, wu, wdn, nn)


def _layer(x, h, b, t, st_in, l_in, st_prev, l, depth, lw, wts, last):
    (lb, n_hgrn, bias_c, bias_r, n_mlstm, wc, wd, bd, lg, lbias, n_mlp, n_next) = lw
    w_in, wb, wo, wu, wdn = wts
    L = math.gcd(t, CHUNK)
    G = min(b, SEQ_GROUP)
    proj = _proj(h, w_in, l)
    proj3 = proj.reshape(b, t, D_PROJ)
    small = proj3[:, :, COL_SMALL:COL_SMALL + 2 * N_HEAD]
    small_row = jnp.swapaxes(small.reshape(b, t // L, L, 2 * N_HEAD), 2, 3)

    prev = (None,) * 6 if st_prev is None else st_prev
    g_h, tt_h = (2, 512) if t >= 512 else (G, t)
    oa, s_hgrn = _hgrn(proj3, lb, n_hgrn, st_in[0], l_in, prev[0], l, depth, L, g_h, tt_h)
    tt_m = min(t, 128)
    ob, s_c, s_n, s_m = _mlstm(proj3, small_row, bias_c, bias_r, n_mlstm, st_in[1:4], l_in,
                               None if st_prev is None else prev[1:4], l, depth, L, G, tt_m)
    g_c, tt_c = (1, 512) if t >= 512 else (G, t)
    oc, od, buf_c, buf_d = _conv(proj3, st_in[4:6], l_in, None if st_prev is None else prev[4:6], l, depth,
                                 wc, wd, bd, lg, lbias, g_c, tt_c)
    flat = lambda a: a.reshape(b * t, D_BR)
    outs = _mix(x, flat(oa), flat(ob), flat(oc), flat(od), proj, wb, wo, n_mlp, wu, wdn, n_next, l, last)
    return outs, (s_hgrn, s_c, s_n, s_m, buf_c, buf_d)


def kernel(x_prompt, x_sample, state_hgrn, state_mlstm_c, state_mlstm_n, state_mlstm_m, state_conv_short, state_conv_conformer, w_in, w_branch, w_out, lb_logits, norm_hgrn, mlstm_i_bias, mlstm_f_bias, norm_mlstm, conv_short_w, conv_conformer_w, conv_conformer_b, ln_conformer_g, ln_conformer_b, norm_mix, norm_mlp, w_up, w_down, norm_final):
    depth = w_in.shape[0]
    bp, tp, _ = x_prompt.shape
    bs, ts, _ = x_sample.shape
    lower_bounds = _lower_bounds(lb_logits)

    wts = (_permute_w_in(jnp.swapaxes(w_in, 1, 2)),) + tuple(a.astype(BF16) for a in (w_branch, w_out, w_up, w_down))
    gate_bias = jnp.concatenate([mlstm_i_bias, mlstm_f_bias], axis=1).astype(F32)
    bias_c = jnp.pad(gate_bias, ((0, 0), (0, N_SMALL - 2 * N_HEAD)))[:, None, :]
    bias_r = gate_bias[:, :, None]
    row = lambda a, l: a[l][None, :].astype(F32)

    zero_st = (jnp.zeros((1, bp, N_HEAD, D_HEAD, D_HEAD), F32), jnp.zeros((1, bp, N_HEAD, D_HEAD, D_HEAD), F32),
               jnp.zeros((1, bp, N_HEAD, 1, D_HEAD), F32), jnp.zeros((1, bp, 1, N_SMALL), F32),
               jnp.zeros((1, bp, CONV_C - 1, D_BR), F32), jnp.zeros((1, bp, CONV_D - 1, D_BR), F32))
    m_lanes = jnp.pad(state_mlstm_m.astype(F32), ((0, 0), (0, 0), (0, N_SMALL - N_HEAD)))[:, :, None, :]
    past_st = (state_hgrn, state_mlstm_c, state_mlstm_n[:, :, :, None, :], m_lanes,
               state_conv_short, state_conv_conformer)

    xp = x_prompt.reshape(bp * tp, D_MODEL)
    xs = x_sample.reshape(bs * ts, D_MODEL)
    hp = _norm(xp, row(norm_mix, 0))
    hs = _norm(xs, row(norm_mix, 0))
    st_p = st_s = None
    for l in range(depth):
        last = l == depth - 1
        n_next = norm_final[None, :].astype(F32) if last else row(norm_mix, l + 1)
        lw = (lower_bounds[l][None, :], row(norm_hgrn, l), bias_c[l], bias_r[l], row(norm_mlstm, l),
              conv_short_w[l], conv_conformer_w[l], row(conv_conformer_b, l), row(ln_conformer_g, l),
              row(ln_conformer_b, l), row(norm_mlp, l), n_next)
        outs_p, st_p = _layer(xp, hp, bp, tp, zero_st, 0, st_p, l, depth, lw, wts, last)
        outs_s, st_s = _layer(xs, hs, bs, ts, past_st, l, st_s, l, depth, lw, wts, last)
        if last:
            xp, xs = outs_p[0], outs_s[0]
        else:
            (xp, hp), (xs, hs) = outs_p, outs_s

    def finish(x, b, t, st):
        s_hgrn, s_c, s_n, s_m, buf_c, buf_d = st
        return (x.reshape(b, t, D_MODEL), s_hgrn, s_c, s_n[:, :, :, 0, :], s_m[:, :, 0, :N_HEAD], buf_c, buf_d)

    fp = finish(xp, bp, tp, st_p)
    fs = finish(xs, bs, ts, st_s)
    outs = []
    for a, c in zip(fp, fs):
        outs += [a, c]
    return tuple(outs)
```

```python
import functools
import math

import jax
import jax.numpy as jnp
from jax import lax
from jax.experimental import pallas as pl
from jax.experimental.pallas import tpu as pltpu

F32 = jnp.float32
BF16 = jnp.bfloat16

D_MODEL = 1024
N_HEAD = 4
D_HEAD = 128
D_BR = 512
N_BRANCH = 4
CONV_C = 3
CONV_D = 31
D_FF = 4 * D_MODEL
CHUNK = 64
EPS = 1e-6
SUBLANES = 8
LANES = 128

N_MAIN = 8 * D_BR
N_GATE = N_BRANCH * D_MODEL
N_CONV = 5 * D_BR
N_SMALL = LANES
D_PROJ = N_MAIN + N_GATE + N_CONV + N_SMALL
COL_GATE = N_MAIN
COL_CONV = N_MAIN + N_GATE
COL_SMALL = N_MAIN + N_GATE + N_CONV
PROJ_TN = D_PROJ // 5

VMEM_LIMIT = 56 * 1024 * 1024
ROW_TILE = 512
SEQ_GROUP = 8


def _cparams(sem, vmem_limit=VMEM_LIMIT):
    return pltpu.CompilerParams(dimension_semantics=sem, vmem_limit_bytes=vmem_limit)


def _dot(a, b):
    return jnp.dot(a.astype(BF16), b.astype(BF16), preferred_element_type=F32)


def _dot_nt(a, b):
    return lax.dot_general(a.astype(BF16), b.astype(BF16), (((1,), (1,)), ((), ())),
                           preferred_element_type=F32)


def _dot_tn(a, b):
    return lax.dot_general(a.astype(BF16), b.astype(BF16), (((0,), (0,)), ((), ())),
                           preferred_element_type=F32)


def _split3(x):
    hi = x.astype(BF16)
    r = x - hi.astype(F32)
    mid = r.astype(BF16)
    lo = (r - mid.astype(F32)).astype(BF16)
    return hi, mid, lo


def _sigmoid(x):
    return jax.nn.sigmoid(x)


def _log_sigmoid(x):
    return jnp.minimum(x, 0.0) - jnp.log1p(jnp.exp(-jnp.abs(x)))


def _rms(x, g):
    return x * lax.rsqrt(jnp.mean(x * x, axis=-1, keepdims=True) + EPS) * g


def _layer_block(shape, l, grid_rank):
    zeros = (0,) * (len(shape) - 1)
    if grid_rank == 2:
        return pl.BlockSpec((None,) + shape, lambda i, j: (l, i) + zeros)
    return pl.BlockSpec((None,) + shape, lambda i: (l, i) + zeros)


def _lb_kernel(x_ref, o_ref):
    x = x_ref[...]
    depth = x.shape[0]
    e = jnp.exp(x - jnp.max(x, axis=0, keepdims=True))
    s = e / jnp.sum(e, axis=0, keepdims=True)
    acc = jnp.zeros_like(s[0:1])
    for i in range(depth):
        acc = acc + s[i:i + 1]
        o_ref[i:i + 1, :] = acc - s[0:1]


def _lower_bounds(lb_logits):
    return pl.pallas_call(
        _lb_kernel, out_shape=jax.ShapeDtypeStruct(lb_logits.shape, F32), name="lower_bounds",
    )(lb_logits.astype(F32))


def _norm_kernel(x_ref, g_ref, o_ref):
    o_ref[...] = _rms(x_ref[...], g_ref[...]).astype(o_ref.dtype)


def _norm(x, g):
    n = x.shape[0]
    tm = min(n, ROW_TILE)
    return pl.pallas_call(
        _norm_kernel,
        grid=(n // tm,),
        in_specs=[pl.BlockSpec((tm, D_MODEL), lambda i: (i, 0)), pl.BlockSpec((1, D_MODEL), lambda i: (0, 0))],
        out_specs=pl.BlockSpec((tm, D_MODEL), lambda i: (i, 0)),
        out_shape=jax.ShapeDtypeStruct((n, D_MODEL), BF16),
        compiler_params=_cparams(("parallel",)),
        name="norm",
    )(x, g)


PERM_LANES = 128


def _permute_kernel(w_ref, o_ref):
    o_small = N_MAIN
    o_conv = o_small + 2 * N_HEAD
    o_gate = o_conv + N_CONV
    o_ref[:N_MAIN, :] = w_ref[:N_MAIN, :].astype(o_ref.dtype)
    o_ref[COL_GATE:COL_GATE + N_GATE, :] = w_ref[o_gate:o_gate + N_GATE, :].astype(o_ref.dtype)
    o_ref[COL_CONV:COL_CONV + N_CONV, :] = w_ref[o_conv:o_conv + N_CONV, :].astype(o_ref.dtype)
    pad = jnp.zeros((N_SMALL - 2 * N_HEAD, o_ref.shape[1]), F32)
    o_ref[COL_SMALL:, :] = jnp.concatenate([w_ref[o_small:o_conv, :], pad], axis=0).astype(o_ref.dtype)


def _permute_w_in(w_in_t):
    depth, d_in, d = w_in_t.shape
    return pl.pallas_call(
        _permute_kernel,
        grid=(depth, d // PERM_LANES),
        in_specs=[pl.BlockSpec((None, d_in, PERM_LANES), lambda l, i: (l, 0, i))],
        out_specs=pl.BlockSpec((None, D_PROJ, PERM_LANES), lambda l, i: (l, 0, i)),
        out_shape=jax.ShapeDtypeStruct((depth, D_PROJ, d), BF16),
        compiler_params=_cparams(("parallel", "parallel")),
        name="permute_w_in",
    )(w_in_t)


def _proj_kernel(h_ref, w_ref, o_ref):
    o_ref[...] = lax.dot_general(h_ref[...], w_ref[...], (((1,), (1,)), ((), ())), preferred_element_type=F32)


def _proj(h, w_t, l):
    n = h.shape[0]
    tm = min(n, 2 * ROW_TILE)
    return pl.pallas_call(
        _proj_kernel,
        grid=(D_PROJ // PROJ_TN, n // tm),
        in_specs=[pl.BlockSpec((tm, D_MODEL), lambda j, i: (i, 0)),
                  pl.BlockSpec((None, PROJ_TN, D_MODEL), lambda j, i: (l, j, 0))],
        out_specs=pl.BlockSpec((tm, PROJ_TN), lambda j, i: (i, j)),
        out_shape=jax.ShapeDtypeStruct((n, D_PROJ), F32),
        compiler_params=_cparams(("parallel", "parallel")),
        name="proj",
    )(h, w_t)


def _cumsum_rows(x):
    n = x.shape[0]
    row = lax.broadcasted_iota(jnp.int32, x.shape, 0)
    s = 1
    while s < n:
        x = x + jnp.where(row >= s, pltpu.roll(x, s, 0), 0.0)
        s *= 2
    return x


def _group_row(x, group, j):
    n, w = x.shape
    x3 = x.reshape(n // group, group, w)
    return jnp.broadcast_to(x3[:, j:j + 1, :], x3.shape).reshape(n, w)


def _boundary_row(x, c, row):
    if 2 * c >= SUBLANES:
        return _group_row(x, 2 * c, c - 1)
    if c == 2:
        return jnp.where(row % SUBLANES < 4, _group_row(x, SUBLANES, 1), _group_row(x, SUBLANES, 5))
    return jnp.where(row % 2 == 1, pltpu.roll(x, 1, 0), x)


def _hgrn_kernel(qa_ref, fa_ref, ia_ref, ga_ref, lb_ref, nw_ref, s0_ref, prev_ref, o_ref, s_ref, *, L):
    del prev_ref
    G, tt, _ = qa_ref.shape
    nchunk = tt // L
    levels = [c for c in (32, 16, 8, 4, 2, 1) if 2 * c <= L]

    @pl.when(pl.program_id(1) == 0)
    def _():
        s_ref[...] = s0_ref[...]

    lb = lb_ref[...]
    nw = nw_ref[...]
    row = lax.broadcasted_iota(jnp.int32, (L, D_BR), 0)
    row_l = lax.broadcasted_iota(jnp.int32, (L, L), 0)
    col_l = lax.broadcasted_iota(jnp.int32, (L, L), 1)
    heads = [slice(h * D_HEAD, (h + 1) * D_HEAD) for h in range(N_HEAD)]
    level = jnp.where(row_l == col_l, 0.0, -1.0)
    for c in levels:
        pair = ((row_l // (2 * c)) == (col_l // (2 * c))) & (row_l % (2 * c) >= c) & (col_l % (2 * c) < c)
        level = jnp.where(pair, float(c), level)

    def body(ci, carry):
        rows = pl.ds(pl.multiple_of(ci * L, L), L)
        seqs = range(G)
        k, q, gc, out, scores = {}, {}, {}, {}, {}
        for bi in seqs:
            f = lb + (1.0 - lb) * _sigmoid(fa_ref[bi, rows, :])
            k[bi] = 1.0 - f
            qa = qa_ref[bi, rows, :]
            q[bi] = qa * _sigmoid(qa)
            gc[bi] = _cumsum_rows(jnp.log(f))
        for bi in seqs:
            qg = q[bi] * jnp.exp(gc[bi])
            for h, hs in enumerate(heads):
                out[bi, h] = _dot(qg[:, hs], s_ref[bi, h])
                scores[bi, h] = jnp.where(level == 0.0, _dot_nt(q[bi][:, hs], k[bi][:, hs]), 0.0)
        for c in levels:
            for bi in seqs:
                diff = gc[bi] - _boundary_row(gc[bi], c, row)
                qe = q[bi] * jnp.exp(diff)
                ke = k[bi] * jnp.exp(-diff)
                for h, hs in enumerate(heads):
                    scores[bi, h] = jnp.where(level == float(c), _dot_nt(qe[:, hs], ke[:, hs]), scores[bi, h])
        for bi in seqs:
            v = ia_ref[bi, rows, :]
            g_last = gc[bi][L - 1:L, :]
            k_dec = k[bi] * jnp.exp(g_last - gc[bi])
            decay = jnp.exp(g_last)
            ga = ga_ref[bi, rows, :]
            gate = ga * _sigmoid(ga)
            for h, hs in enumerate(heads):
                o = out[bi, h] + _dot(scores[bi, h], v[:, hs])
                decay_col = jnp.broadcast_to(decay[:, hs], (D_HEAD, D_HEAD)).T
                s_ref[bi, h] = decay_col * s_ref[bi, h] + _dot_tn(k_dec[:, hs], v[:, hs])
                y = o * lax.rsqrt(jnp.mean(o * o, axis=-1, keepdims=True) + EPS) * nw[:, hs]
                o_ref[bi, rows, hs] = (y * gate[:, hs]).astype(o_ref.dtype)
        return carry

    lax.fori_loop(0, nchunk, body, 0)


def _hgrn(proj3, lb, nw, s_in, l_in, s_prev, l, depth, L, G, tt):
    b, t, _ = proj3.shape
    col = lambda c: pl.BlockSpec((G, tt, D_BR), lambda i, j: (i, j, c))
    vec = pl.BlockSpec((1, D_BR), lambda i, j: (0, 0))
    st = (G, N_HEAD, D_HEAD, D_HEAD)
    aliases = {}
    if s_prev is None:
        s_prev = jnp.zeros((1,), F32)
    else:
        aliases = {7: 1}
    return pl.pallas_call(
        functools.partial(_hgrn_kernel, L=L),
        grid=(b // G, t // tt),
        in_specs=[col(0), col(1), col(2), col(3), vec, vec, _layer_block(st, l_in, 2),
                  pl.BlockSpec(memory_space=pl.ANY)],
        out_specs=[pl.BlockSpec((G, tt, D_BR), lambda i, j: (i, j, 0)), _layer_block(st, l, 2)],
        out_shape=[jax.ShapeDtypeStruct((b, t, D_BR), BF16),
                   jax.ShapeDtypeStruct((depth, b) + st[1:], F32)],
        input_output_aliases=aliases,
        compiler_params=_cparams(("parallel", "arbitrary")),
        name="hgrn",
    )(proj3, proj3, proj3, proj3, lb, nw, s_in, s_prev)


def _mlstm_kernel(q_ref, k_ref, v_ref, og_ref, sc_ref, sr_ref, bc_ref, br_ref, nw_ref,
                  c0_ref, n0_ref, m0_ref, cp_ref, np_ref, mp_ref, o_ref, c_ref, n_ref, m_ref, *, L):
    del cp_ref, np_ref, mp_ref
    G, tt, _ = q_ref.shape
    nchunk = tt // L

    @pl.when(pl.program_id(1) == 0)
    def _():
        c_ref[...] = c0_ref[...]
        n_ref[...] = n0_ref[...]
        m_ref[...] = m0_ref[...]

    nw = nw_ref[...]
    bias_c = bc_ref[...]
    bias_r = br_ref[...]
    row_l = lax.broadcasted_iota(jnp.int32, (L, L), 0)
    col_l = lax.broadcasted_iota(jnp.int32, (L, L), 1)
    causal = row_l >= col_l
    tri = jnp.where(causal, 1.0, 0.0).astype(BF16)
    tri_t = jnp.where(row_l <= col_l, 1.0, 0.0).astype(BF16)
    row_c = lax.broadcasted_iota(jnp.int32, (L, N_SMALL), 0)
    ones = jnp.ones((L, D_HEAD), F32)
    scale = D_HEAD ** -0.5
    heads = [slice(h * D_HEAD, (h + 1) * D_HEAD) for h in range(N_HEAD)]
    group = G

    def gates(bi, ci, rows):
        pre_c = sc_ref[bi, rows, :] + bias_c
        pre_r = sr_ref[bi, ci] + bias_r
        hi, mid, lo = _split3(_log_sigmoid(pre_c))
        b_c = (jnp.dot(tri, hi, preferred_element_type=F32) + jnp.dot(tri, mid, preferred_element_type=F32)
               + jnp.dot(tri, lo, preferred_element_type=F32))
        hi, mid, lo = _split3(_log_sigmoid(pre_r))
        b_r = (jnp.dot(hi, tri_t, preferred_element_type=F32) + jnp.dot(mid, tri_t, preferred_element_type=F32)
               + jnp.dot(lo, tri_t, preferred_element_type=F32))
        b = pltpu.roll(b_c, N_SMALL - N_HEAD, 1)
        a = pre_c - b
        cm = a
        s = 1
        while s < L:
            cm = jnp.maximum(cm, jnp.where(row_c >= s, pltpu.roll(cm, s, 0), -jnp.inf))
            s *= 2
        m_old = m_ref[bi]
        big_m = jnp.maximum(m_old, cm)
        m_last = big_m[L - 1:L, :]
        m_ref[bi] = b[L - 1:L, :] + m_last
        return dict(a_r=pre_r[0:N_HEAD, :] - b_r[N_HEAD:2 * N_HEAD, :], big_m=big_m, m_old=m_old,
                    w_inter=jnp.exp(m_old - big_m), n_exp=jnp.exp(-(b + big_m)),
                    w_state=jnp.exp(a - m_last), carry=jnp.exp(m_old - m_last))

    def body(ci, carry):
        rows = pl.ds(pl.multiple_of(ci * L, L), L)
        for g0 in range(0, G, group):
            seqs = range(g0, g0 + group)
            sh = {bi: gates(bi, ci, rows) for bi in seqs}
            probs = [(bi, h) for bi in seqs for h in range(N_HEAD)]
            qk, wi_b, intra, qc, qn = {}, {}, {}, {}, {}
            for p in probs:
                bi, h = p
                g = sh[bi]
                m_b = jnp.broadcast_to(g["big_m"][:, h:h + 1], (L, D_HEAD))
                wi_b[p] = jnp.exp(g["m_old"][:, h:h + 1] - m_b)
                w = jnp.where(causal, jnp.exp(g["a_r"][h:h + 1, :] - m_b[:, :L]), 0.0)
                q = q_ref[bi, rows, heads[h]] * scale
                qk[p] = _dot_nt(q, k_ref[bi, rows, heads[h]]) * w
            for p in probs:
                bi, h = p
                q = q_ref[bi, rows, heads[h]] * scale
                intra[p] = _dot(qk[p], jnp.concatenate([v_ref[bi, rows, heads[h]], ones], axis=1))
                qc[p] = _dot(q, c_ref[bi, h])
                qn[p] = _dot_nt(q, jnp.broadcast_to(n_ref[bi, h], (D_HEAD, D_HEAD)))
            for p in probs:
                bi, h = p
                g = sh[bi]
                num = intra[p][:, :D_HEAD] + wi_b[p] * qc[p]
                den = intra[p][:, D_HEAD + h:D_HEAD + h + 1] + g["w_inter"][:, h:h + 1] * qn[p][:, h:h + 1]
                r = 1.0 / jnp.maximum(jnp.abs(den), g["n_exp"][:, h:h + 1])
                t = r * lax.rsqrt(r * r * jnp.mean(num * num, axis=-1, keepdims=True) + EPS)
                gate = _sigmoid(og_ref[bi, rows, heads[h]])
                o_ref[bi, rows, heads[h]] = (num * t * nw[:, heads[h]] * gate).astype(o_ref.dtype)
            for p in probs:
                bi, h = p
                g = sh[bi]
                k = k_ref[bi, rows, heads[h]]
                kw = g["w_state"][:, h:h + 1] * k
                cd = g["carry"][:, h:h + 1]
                c_ref[bi, h] = cd * c_ref[bi, h] + _dot_tn(kw, v_ref[bi, rows, heads[h]])
                n_ref[bi, h] = cd * n_ref[bi, h] + jnp.sum(kw, axis=0, keepdims=True)
        return carry

    lax.fori_loop(0, nchunk, body, 0)


def _mlstm(proj3, small_row, bias_c, bias_r, nw, st_in, l_in, st_prev, l, depth, L, G, tt):
    b, t, _ = proj3.shape
    col = lambda c: pl.BlockSpec((G, tt, D_BR), lambda i, j: (i, j, c))
    full2 = lambda shape: pl.BlockSpec(shape, lambda i, j: (0, 0))
    shapes = [(G, N_HEAD, D_HEAD, D_HEAD), (G, N_HEAD, 1, D_HEAD), (G, 1, N_SMALL)]
    aliases = {}
    if st_prev is None:
        st_prev = [jnp.zeros((1,), F32)] * 3
    else:
        aliases = {12: 1, 13: 2, 14: 3}
    any_spec = pl.BlockSpec(memory_space=pl.ANY)
    return pl.pallas_call(
        functools.partial(_mlstm_kernel, L=L),
        grid=(b // G, t // tt),
        in_specs=[col(4), col(5), col(6), col(7),
                  pl.BlockSpec((G, tt, N_SMALL), lambda i, j: (i, j, COL_SMALL // N_SMALL)),
                  pl.BlockSpec((G, tt // L, 2 * N_HEAD, L), lambda i, j: (i, j, 0, 0)),
                  full2((1, N_SMALL)), full2((2 * N_HEAD, 1)), full2((1, D_BR))]
                 + [_layer_block(s, l_in, 2) for s in shapes] + [any_spec] * 3,
        out_specs=[pl.BlockSpec((G, tt, D_BR), lambda i, j: (i, j, 0))] + [_layer_block(s, l, 2) for s in shapes],
        out_shape=[jax.ShapeDtypeStruct((b, t, D_BR), BF16)]
                  + [jax.ShapeDtypeStruct((depth, b) + s[1:], F32) for s in shapes],
        input_output_aliases=aliases,
        compiler_params=_cparams(("parallel", "arbitrary")),
        name="mlstm",
    )(proj3, proj3, proj3, proj3, proj3, small_row, bias_c, bias_r, nw, *st_in, *st_prev)


PAD_C = 8
PAD_D = 32
CONV_ROWS = 32


def _conv_kernel(bc_ref, cc_ref, xc_ref, ad_ref, gd_ref, bufc_ref, bufd_ref, wc_ref, wd_ref, bd_ref,
                 lg_ref, lbias_ref, pc_ref, pd_ref, oc_ref, od_ref, nbc_ref, nbd_ref, extc, extd, shc, shd):
    del pc_ref, pd_ref
    bblk, tt, _ = bc_ref.shape
    hc, hd = CONV_C - 1, CONV_D - 1
    rt = min(tt, CONV_ROWS)
    offs_c = [PAD_C - hc + j for j in range(CONV_C)]
    offs_d = [PAD_D - hd + j for j in range(CONV_D)]
    unaligned_c = [o for o in offs_c if o % SUBLANES]
    span_d = tt + PAD_D - SUBLANES

    @pl.when(pl.program_id(1) == 0)
    def _():
        extc[:, PAD_C - hc:PAD_C, :] = bufc_ref[...]
        extd[:, PAD_D - hd:PAD_D, :] = bufd_ref[...]

    extc[:, PAD_C:PAD_C + tt, :] = cc_ref[...] * xc_ref[...]
    extd[:, PAD_D:PAD_D + tt, :] = ad_ref[...] * _sigmoid(gd_ref[...])

    def tile_rows(w8):
        return w8 if rt == SUBLANES else jnp.concatenate([w8] * (rt // SUBLANES), axis=0)

    def seq_body(bi, carry):
        for res in range(1, SUBLANES):
            shd[res, 0:span_d, :] = extd[bi, res:res + span_d, :]
        for i, o in enumerate(unaligned_c):
            shc[i, 0:tt, :] = extc[bi, o:o + tt, :]

        def tile_body(ti, c2):
            r0 = pl.multiple_of(ti * rt, rt)
            acc = jnp.zeros((rt, D_BR), F32)
            for j, o in enumerate(offs_c):
                if o % SUBLANES:
                    x = shc[unaligned_c.index(o), pl.ds(r0, rt), :]
                else:
                    x = extc[bi, pl.ds(pl.multiple_of(r0 + o, SUBLANES), rt), :]
                acc = acc + tile_rows(wc_ref[j]) * x
            oc_ref[bi, pl.ds(r0, rt), :] = (bc_ref[bi, pl.ds(r0, rt), :] * acc).astype(oc_ref.dtype)
            acc = jnp.zeros((rt, D_BR), F32)
            for j, o in enumerate(offs_d):
                res = o % SUBLANES
                rows = pl.ds(pl.multiple_of(r0 + (o - res), SUBLANES), rt)
                x = shd[res, rows, :] if res else extd[bi, rows, :]
                acc = acc + tile_rows(wd_ref[j]) * x
            y = acc + bd_ref[...]
            yc = y - jnp.mean(y, axis=-1, keepdims=True)
            z = yc * lax.rsqrt(jnp.mean(yc * yc, axis=-1, keepdims=True) + EPS) * lg_ref[...] + lbias_ref[...]
            od_ref[bi, pl.ds(r0, rt), :] = (z * _sigmoid(z)).astype(od_ref.dtype)
            return c2

        lax.fori_loop(0, tt // rt, tile_body, 0, unroll=4 if (tt // rt) % 4 == 0 else 1)
        return carry

    lax.fori_loop(0, bblk, seq_body, 0)

    new_c = extc[:, PAD_C + tt - hc:PAD_C + tt, :]
    new_d = extd[:, PAD_D + tt - hd:PAD_D + tt, :]
    nbc_ref[...] = new_c
    nbd_ref[...] = new_d
    extc[:, PAD_C - hc:PAD_C, :] = new_c
    extd[:, PAD_D - hd:PAD_D, :] = new_d


def _conv(proj3, buf_in, l_in, buf_prev, l, depth, wc, wd, bd, lg, lbias, bblk, tt):
    b, t, _ = proj3.shape
    c0 = COL_CONV // D_BR
    col = lambda c: pl.BlockSpec((bblk, tt, D_BR), lambda i, j: (i, j, c0 + c))
    full2 = lambda shape: pl.BlockSpec(shape, lambda i, j: (0, 0))
    full3 = lambda shape: pl.BlockSpec(shape, lambda i, j: (0, 0, 0))
    rows8 = lambda w: jnp.broadcast_to(w[:, None, :], (w.shape[0], SUBLANES, D_BR))
    shapes = [(bblk, CONV_C - 1, D_BR), (bblk, CONV_D - 1, D_BR)]
    outc =pl.BlockSpec((bblk, tt, D_BR), lambda i, j: (i, j, 0))
    aliases = {}
    if buf_prev is None:
        buf_prev = [jnp.zeros((1,), F32)] * 2
    else:
        aliases = {12: 2, 13: 3}
    any_spec = pl.BlockSpec(memory_space=pl.ANY)
    return pl.pallas_call(
        _conv_kernel,
        grid=(b // bblk, t // tt),
        in_specs=[col(0), col(1), col(2), col(3), col(4)] + [_layer_block(s, l_in, 2) for s in shapes]
                 + [full3((CONV_C, SUBLANES, D_BR)), full3((CONV_D, SUBLANES, D_BR)), full2((1, D_BR)),
                    full2((1, D_BR)), full2((1, D_BR)), any_spec, any_spec],
        out_specs=[outc, outc] + [_layer_block(s, l, 2) for s in shapes],
        out_shape=[jax.ShapeDtypeStruct((b, t, D_BR), BF16), jax.ShapeDtypeStruct((b, t, D_BR), BF16)]
                  + [jax.ShapeDtypeStruct((depth, b) + s[1:], F32) for s in shapes],
        scratch_shapes=[pltpu.VMEM((bblk, PAD_C + tt, D_BR), F32), pltpu.VMEM((bblk, PAD_D + tt, D_BR), F32),
                        pltpu.VMEM((CONV_C - 1, tt, D_BR), F32), pltpu.VMEM((SUBLANES, PAD_D + tt, D_BR), F32)],
        input_output_aliases=aliases,
        compiler_params=_cparams(("parallel", "arbitrary")),
        name="conv",
    )(proj3, proj3, proj3, proj3, proj3, *buf_in, rows8(wc), rows8(wd), bd, lg, lbias, *buf_prev)


FF_TILE = 1024
MIX_VMEM_LIMIT = 60 * 1024 * 1024


def _mix_kernel(x_ref, oa_ref, ob_ref, oc_ref, od_ref, gt_ref, wb_ref, wo_ref, nm_ref, wu_ref, wdn_ref,
                nn_ref, *out_refs, last):
    merged = None
    for n, br in enumerate((oa_ref, ob_ref, oc_ref, od_ref)):
        gate = _sigmoid(gt_ref[:, n * D_MODEL:(n + 1) * D_MODEL])
        term = gate * jnp.dot(br[...], wb_ref[n], preferred_element_type=F32)
        merged = term if merged is None else merged + term
    x = x_ref[...] + jnp.dot(merged.astype(BF16), wo_ref[...], preferred_element_type=F32)
    hm = _rms(x, nm_ref[...]).astype(BF16)
    for c in range(D_FF // FF_TILE):
        cs = slice(c * FF_TILE, (c + 1) * FF_TILE)
        up = jnp.maximum(jnp.dot(hm, wu_ref[:, cs], preferred_element_type=F32), 0.0)
        x = x + jnp.dot((up * up).astype(BF16), wdn_ref[cs, :], preferred_element_type=F32)
    if last:
        out_refs[0][...] = _rms(x, nn_ref[...])
    else:
        out_refs[0][...] = x
        out_refs[1][...] = _rms(x, nn_ref[...]).astype(BF16)


def _mix(x, oa, ob, oc, od, proj, wb, wo, nm, wu, wdn, nn, l, last):
    n = x.shape[0]
    tm = min(n, ROW_TILE)
    row = lambda w: pl.BlockSpec((tm, w), lambda i: (i, 0))
    once = pl.Buffered(1)
    full2 = lambda shape: pl.BlockSpec(shape, lambda i: (0, 0))
    layer3 = lambda shape: pl.BlockSpec((None,) + shape, lambda i: (l, 0, 0), pipeline_mode=once)
    out_specs = [row(D_MODEL)] if last else [row(D_MODEL), row(D_MODEL)]
    out_shape = [jax.ShapeDtypeStruct((n, D_MODEL), F32)]
    if not last:
        out_shape.append(jax.ShapeDtypeStruct((n, D_MODEL), BF16))
    return pl.pallas_call(
        functools.partial(_mix_kernel, last=last),
        grid=(n // tm,),
        in_specs=[row(D_MODEL), row(D_BR), row(D_BR), row(D_BR), row(D_BR),
                  pl.BlockSpec((tm, N_GATE), lambda i: (i, COL_GATE // N_GATE)),
                  pl.BlockSpec((None, N_BRANCH, D_BR, D_MODEL), lambda i: (l, 0, 0, 0), pipeline_mode=once),
                  layer3((D_MODEL, D_MODEL)), full2((1, D_MODEL)), layer3((D_MODEL, D_FF)),
                  layer3((D_FF, D_MODEL)), full2((1, D_MODEL))],
        out_specs=out_specs,
        out_shape=out_shape,
        compiler_params=_cparams(("parallel",), MIX_VMEM_LIMIT),
        name="mix",
    )(x, oa, ob, oc, od, proj, wb, wo, nm, wu, wdn, nn)


def _layer(x, h, b, t, st_in, l_in, st_prev, l, depth, lw, wts, last):
    (lb, n_hgrn, bias_c, bias_r, n_mlstm, wc, wd, bd, lg, lbias, n_mlp, n_next) = lw
    w_in, wb, wo, wu, wdn = wts
    L = math.gcd(t, CHUNK)
    G = min(b, SEQ_GROUP)
    proj = _proj(h, w_in, l)
    proj3 = proj.reshape(b, t, D_PROJ)
    small = proj3[:, :, COL_SMALL:COL_SMALL + 2 * N_HEAD]
    small_row = jnp.swapaxes(small.reshape(b, t // L, L, 2 * N_HEAD), 2, 3)

    prev = (None,) * 6 if st_prev is None else st_prev
    g_h, tt_h = (2, 512) if t >= 512 else (G, t)
    oa, s_hgrn = _hgrn(proj3, lb, n_hgrn, st_in[0], l_in, prev[0], l, depth, L, g_h, tt_h)
    tt_m = min(t, 128)
    ob, s_c, s_n, s_m = _mlstm(proj3, small_row, bias_c, bias_r, n_mlstm, st_in[1:4], l_in,
                               None if st_prev is None else prev[1:4], l, depth, L, G, tt_m)
    g_c, tt_c = (1, 512) if t >= 512 else (min(b, 4 * SEQ_GROUP), t)
    oc, od, buf_c, buf_d = _conv(proj3, st_in[4:6], l_in, None if st_prev is None else prev[4:6], l, depth,
                                 wc, wd, bd, lg, lbias, g_c, tt_c)
    flat = lambda a: a.reshape(b * t, D_BR)
    outs = _mix(x, flat(oa), flat(ob), flat(oc), flat(od), proj, wb, wo, n_mlp, wu, wdn, n_next, l, last)
    return outs, (s_hgrn, s_c, s_n, s_m, buf_c, buf_d)


def kernel(x_prompt, x_sample, state_hgrn, state_mlstm_c, state_mlstm_n, state_mlstm_m, state_conv_short, state_conv_conformer, w_in, w_branch, w_out, lb_logits, norm_hgrn, mlstm_i_bias, mlstm_f_bias, norm_mlstm, conv_short_w, conv_conformer_w, conv_conformer_b, ln_conformer_g, ln_conformer_b, norm_mix, norm_mlp, w_up, w_down, norm_final):
    depth = w_in.shape[0]
    bp, tp, _ = x_prompt.shape
    bs, ts, _ = x_sample.shape
    lower_bounds = _lower_bounds(lb_logits)

    wts = (_permute_w_in(jnp.swapaxes(w_in, 1, 2)),) + tuple(a.astype(BF16) for a in (w_branch, w_out, w_up, w_down))
    gate_bias = jnp.concatenate([mlstm_i_bias, mlstm_f_bias], axis=1).astype(F32)
    bias_c = jnp.pad(gate_bias, ((0, 0), (0, N_SMALL - 2 * N_HEAD)))[:, None, :]
    bias_r = gate_bias[:, :, None]
    row = lambda a, l: a[l][None, :].astype(F32)

    zero_st = (jnp.zeros((1, bp, N_HEAD, D_HEAD, D_HEAD), F32), jnp.zeros((1, bp, N_HEAD, D_HEAD, D_HEAD), F32),
               jnp.zeros((1, bp, N_HEAD, 1, D_HEAD), F32), jnp.zeros((1, bp, 1, N_SMALL), F32),
               jnp.zeros((1, bp, CONV_C - 1, D_BR), F32), jnp.zeros((1, bp, CONV_D - 1, D_BR), F32))
    m_lanes = jnp.pad(state_mlstm_m.astype(F32), ((0, 0), (0, 0), (0, N_SMALL - N_HEAD)))[:, :, None, :]
    past_st = (state_hgrn, state_mlstm_c, state_mlstm_n[:, :, :, None, :], m_lanes,
               state_conv_short, state_conv_conformer)

    xp = x_prompt.reshape(bp * tp, D_MODEL)
    xs = x_sample.reshape(bs * ts, D_MODEL)
    hp = _norm(xp, row(norm_mix, 0))
    hs = _norm(xs, row(norm_mix, 0))
    st_p = st_s = None
    for l in range(depth):
        last = l == depth - 1
        n_next = norm_final[None, :].astype(F32) if last else row(norm_mix, l + 1)
        lw = (lower_bounds[l][None, :], row(norm_hgrn, l), bias_c[l], bias_r[l], row(norm_mlstm, l),
              conv_short_w[l], conv_conformer_w[l], row(conv_conformer_b, l), row(ln_conformer_g, l),
              row(ln_conformer_b, l), row(norm_mlp, l), n_next)
        outs_p, st_p = _layer(xp, hp, bp, tp, zero_st, 0, st_p, l, depth, lw, wts, last)
        outs_s, st_s = _layer(xs, hs, bs, ts, past_st, l, st_s, l, depth, lw, wts, last)
        if last:
            xp, xs = outs_p[0], outs_s[0]
        else:
            (xp, hp), (xs, hs) = outs_p, outs_s

    def finish(x, b, t, st):
        s_hgrn, s_c, s_n, s_m, buf_c, buf_d = st
        return (x.reshape(b, t, D_MODEL), s_hgrn, s_c, s_n[:, :, :, 0, :], s_m[:, :, 0, :N_HEAD], buf_c, buf_d)

    fp = finish(xp, bp, tp, st_p)
    fs = finish(xs, bs, ts, st_s)
    outs = []
    for a, c in zip(fp, fs):
        outs += [a, c]
    return tuple(outs)
```

```python
import functools
import math

import jax
import jax.numpy as jnp
from jax import lax
from jax.experimental import pallas as pl
from jax.experimental.pallas import tpu as pltpu

F32 = jnp.float32
BF16 = jnp.bfloat16

D_MODEL = 1024
N_HEAD = 4
D_HEAD = 128
D_BR = 512
N_BRANCH = 4
CONV_C = 3
CONV_D = 31
D_FF = 4 * D_MODEL
CHUNK = 64
EPS = 1e-6
SUBLANES = 8
LANES = 128

N_MAIN = 8 * D_BR
N_GATE = N_BRANCH * D_MODEL
N_CONV = 5 * D_BR
N_SMALL = LANES
D_PROJ = N_MAIN + N_GATE + N_CONV + N_SMALL
COL_GATE = N_MAIN
COL_CONV = N_MAIN + N_GATE
COL_SMALL = N_MAIN + N_GATE + N_CONV
PROJ_TN = D_PROJ // 5

VMEM_LIMIT = 56 * 1024 * 1024
ROW_TILE = 512
SEQ_GROUP = 8


def _cparams(sem, vmem_limit=VMEM_LIMIT):
    return pltpu.CompilerParams(dimension_semantics=sem, vmem_limit_bytes=vmem_limit)


def _dot(a, b):
    return jnp.dot(a.astype(BF16), b.astype(BF16), preferred_element_type=F32)


def _dot_nt(a, b):
    return lax.dot_general(a.astype(BF16), b.astype(BF16), (((1,), (1,)), ((), ())),
                           preferred_element_type=F32)


def _dot_tn(a, b):
    return lax.dot_general(a.astype(BF16), b.astype(BF16), (((0,), (0,)), ((), ())),
                           preferred_element_type=F32)


def _split3(x):
    hi = x.astype(BF16)
    r = x - hi.astype(F32)
    mid = r.astype(BF16)
    lo = (r - mid.astype(F32)).astype(BF16)
    return hi, mid, lo


def _sigmoid(x):
    return jax.nn.sigmoid(x)


def _log_sigmoid(x):
    return jnp.minimum(x, 0.0) - jnp.log1p(jnp.exp(-jnp.abs(x)))


def _rms(x, g):
    return x * lax.rsqrt(jnp.mean(x * x, axis=-1, keepdims=True) + EPS) * g


def _layer_block(shape, l, grid_rank):
    zeros = (0,) * (len(shape) - 1)
    if grid_rank == 2:
        return pl.BlockSpec((None,) + shape, lambda i, j: (l, i) + zeros)
    return pl.BlockSpec((None,) + shape, lambda i: (l, i) + zeros)


def _lb_kernel(x_ref, o_ref):
    x = x_ref[...]
    depth = x.shape[0]
    e = jnp.exp(x - jnp.max(x, axis=0, keepdims=True))
    s = e / jnp.sum(e, axis=0, keepdims=True)
    acc = jnp.zeros_like(s[0:1])
    for i in range(depth):
        acc = acc + s[i:i + 1]
        o_ref[i:i + 1, :] = acc - s[0:1]


def _lower_bounds(lb_logits):
    return pl.pallas_call(
        _lb_kernel, out_shape=jax.ShapeDtypeStruct(lb_logits.shape, F32), name="lower_bounds",
    )(lb_logits.astype(F32))


def _norm_kernel(x_ref, g_ref, o_ref):
    o_ref[...] = _rms(x_ref[...], g_ref[...]).astype(o_ref.dtype)


def _norm(x, g):
    n = x.shape[0]
    tm = min(n, ROW_TILE)
    return pl.pallas_call(
        _norm_kernel,
        grid=(n // tm,),
        in_specs=[pl.BlockSpec((tm, D_MODEL), lambda i: (i, 0)), pl.BlockSpec((1, D_MODEL), lambda i: (0, 0))],
        out_specs=pl.BlockSpec((tm, D_MODEL), lambda i: (i, 0)),
        out_shape=jax.ShapeDtypeStruct((n, D_MODEL), BF16),
        compiler_params=_cparams(("parallel",)),
        name="norm",
    )(x, g)


PERM_LANES = 128


def _permute_kernel(w_ref, o_ref):
    o_small = N_MAIN
    o_conv = o_small + 2 * N_HEAD
    o_gate = o_conv + N_CONV
    o_ref[:N_MAIN, :] = w_ref[:N_MAIN, :].astype(o_ref.dtype)
    o_ref[COL_GATE:COL_GATE + N_GATE, :] = w_ref[o_gate:o_gate + N_GATE, :].astype(o_ref.dtype)
    o_ref[COL_CONV:COL_CONV + N_CONV, :] = w_ref[o_conv:o_conv + N_CONV, :].astype(o_ref.dtype)
    pad = jnp.zeros((N_SMALL - 2 * N_HEAD, o_ref.shape[1]), F32)
    o_ref[COL_SMALL:, :] = jnp.concatenate([w_ref[o_small:o_conv, :], pad], axis=0).astype(o_ref.dtype)


def _permute_w_in(w_in_t):
    depth, d_in, d = w_in_t.shape
    return pl.pallas_call(
        _permute_kernel,
        grid=(depth, d // PERM_LANES),
        in_specs=[pl.BlockSpec((None, d_in, PERM_LANES), lambda l, i: (l, 0, i))],
        out_specs=pl.BlockSpec((None, D_PROJ, PERM_LANES), lambda l, i: (l, 0, i)),
        out_shape=jax.ShapeDtypeStruct((depth, D_PROJ, d), BF16),
        compiler_params=_cparams(("parallel", "parallel")),
        name="permute_w_in",
    )(w_in_t)


def _proj_kernel(h_ref, w_ref, o_ref):
    o_ref[...] = lax.dot_general(h_ref[...], w_ref[...], (((1,), (1,)), ((), ())), preferred_element_type=F32)


def _proj(h, w_t, l):
    n = h.shape[0]
    tm = min(n, 2 * ROW_TILE)
    return pl.pallas_call(
        _proj_kernel,
        grid=(D_PROJ // PROJ_TN, n // tm),
        in_specs=[pl.BlockSpec((tm, D_MODEL), lambda j, i: (i, 0)),
                  pl.BlockSpec((None, PROJ_TN, D_MODEL), lambda j, i: (l, j, 0))],
        out_specs=pl.BlockSpec((tm, PROJ_TN), lambda j, i: (i, j)),
        out_shape=jax.ShapeDtypeStruct((n, D_PROJ), F32),
        compiler_params=_cparams(("parallel", "parallel")),
        name="proj",
    )(h, w_t)


def _cumsum_rows(x):
    n = x.shape[0]
    row = lax.broadcasted_iota(jnp.int32, x.shape, 0)
    s = 1
    while s < n:
        x = x + jnp.where(row >= s, pltpu.roll(x, s, 0), 0.0)
        s *= 2
    return x


def _group_row(x, group, j):
    n, w = x.shape
    x3 = x.reshape(n // group, group, w)
    return jnp.broadcast_to(x3[:, j:j + 1, :], x3.shape).reshape(n, w)


def _boundary_row(x, c, row):
    if 2 * c >= SUBLANES:
        return _group_row(x, 2 * c, c - 1)
    if c == 2:
        return jnp.where(row % SUBLANES < 4, _group_row(x, SUBLANES, 1), _group_row(x, SUBLANES, 5))
    return jnp.where(row % 2 == 1, pltpu.roll(x, 1, 0), x)


def _hgrn_kernel(qa_ref, fa_ref, ia_ref, ga_ref, lb_ref, nw_ref, s0_ref, prev_ref, o_ref, s_ref, *, L):
    del prev_ref
    G, tt, _ = qa_ref.shape
    nchunk = tt // L
    levels = [c for c in (32, 16, 8, 4, 2, 1) if 2 * c <= L]

    @pl.when(pl.program_id(1) == 0)
    def _():
        s_ref[...] = s0_ref[...]

    lb = lb_ref[...]
    nw = nw_ref[...]
    row = lax.broadcasted_iota(jnp.int32, (L, D_BR), 0)
    row_l = lax.broadcasted_iota(jnp.int32, (L, L), 0)
    col_l = lax.broadcasted_iota(jnp.int32, (L, L), 1)
    heads = [slice(h * D_HEAD, (h + 1) * D_HEAD) for h in range(N_HEAD)]
    level = jnp.where(row_l == col_l, 0.0, -1.0)
    for c in levels:
        pair = ((row_l // (2 * c)) == (col_l // (2 * c))) & (row_l % (2 * c) >= c) & (col_l % (2 * c) < c)
        level = jnp.where(pair, float(c), level)

    def body(ci, carry):
        rows = pl.ds(pl.multiple_of(ci * L, L), L)
        seqs = range(G)
        k, q, gc, out, scores = {}, {}, {}, {}, {}
        for bi in seqs:
            f = lb + (1.0 - lb) * _sigmoid(fa_ref[bi, rows, :])
            k[bi] = 1.0 - f
            qa = qa_ref[bi, rows, :]
            q[bi] = qa * _sigmoid(qa)
            gc[bi] = _cumsum_rows(jnp.log(f))
        for bi in seqs:
            qg = q[bi] * jnp.exp(gc[bi])
            for h, hs in enumerate(heads):
                out[bi, h] = _dot(qg[:, hs], s_ref[bi, h])
                scores[bi, h] = jnp.where(level == 0.0, _dot_nt(q[bi][:, hs], k[bi][:, hs]), 0.0)
        for c in levels:
            for bi in seqs:
                diff = gc[bi] - _boundary_row(gc[bi], c, row)
                qe = q[bi] * jnp.exp(diff)
                ke = k[bi] * jnp.exp(-diff)
                for h, hs in enumerate(heads):
                    scores[bi, h] = jnp.where(level == float(c), _dot_nt(qe[:, hs], ke[:, hs]), scores[bi, h])
        for bi in seqs:
            v = ia_ref[bi, rows, :]
            g_last = gc[bi][L - 1:L, :]
            k_dec = k[bi] * jnp.exp(g_last - gc[bi])
            decay = jnp.exp(g_last)
            ga = ga_ref[bi, rows, :]
            gate = ga * _sigmoid(ga)
            for h, hs in enumerate(heads):
                o = out[bi, h] + _dot(scores[bi, h], v[:, hs])
                decay_col = jnp.broadcast_to(decay[:, hs], (D_HEAD, D_HEAD)).T
                s_ref[bi, h] = decay_col * s_ref[bi, h] + _dot_tn(k_dec[:, hs], v[:, hs])
                y = o * lax.rsqrt(jnp.mean(o * o, axis=-1, keepdims=True) + EPS) * nw[:, hs]
                o_ref[bi, rows, hs] = (y * gate[:, hs]).astype(o_ref.dtype)
        return carry

    lax.fori_loop(0, nchunk, body, 0)


def _hgrn(proj3, lb, nw, s_in, l_in, s_prev, l, depth, L, G, tt):
    b, t, _ = proj3.shape
    col = lambda c: pl.BlockSpec((G, tt, D_BR), lambda i, j: (i, j, c))
    vec = pl.BlockSpec((1, D_BR), lambda i, j: (0, 0))
    st = (G, N_HEAD, D_HEAD, D_HEAD)
    aliases = {}
    if s_prev is None:
        s_prev = jnp.zeros((1,), F32)
    else:
        aliases = {7: 1}
    return pl.pallas_call(
        functools.partial(_hgrn_kernel, L=L),
        grid=(b // G, t // tt),
        in_specs=[col(0), col(1), col(2), col(3), vec, vec, _layer_block(st, l_in, 2),
                  pl.BlockSpec(memory_space=pl.ANY)],
        out_specs=[pl.BlockSpec((G, tt, D_BR), lambda i, j: (i, j, 0)), _layer_block(st, l, 2)],
        out_shape=[jax.ShapeDtypeStruct((b, t, D_BR), BF16),
                   jax.ShapeDtypeStruct((depth, b) + st[1:], F32)],
        input_output_aliases=aliases,
        compiler_params=_cparams(("parallel", "arbitrary")),
        name="hgrn",
    )(proj3, proj3, proj3, proj3, lb, nw, s_in, s_prev)


def _mlstm_kernel(q_ref, k_ref, v_ref, og_ref, sc_ref, sr_ref, bc_ref, br_ref, nw_ref,
                  c0_ref, n0_ref, m0_ref, cp_ref, np_ref, mp_ref, o_ref, c_ref, n_ref, m_ref, *, L):
    del cp_ref, np_ref, mp_ref
    G, tt, _ = q_ref.shape
    nchunk = tt // L

    @pl.when(pl.program_id(1) == 0)
    def _():
        c_ref[...] = c0_ref[...]
        n_ref[...] = n0_ref[...]
        m_ref[...] = m0_ref[...]

    nw = nw_ref[...]
    bias_c = bc_ref[...]
    bias_r = br_ref[...]
    row_l = lax.broadcasted_iota(jnp.int32, (L, L), 0)
    col_l = lax.broadcasted_iota(jnp.int32, (L, L), 1)
    causal = row_l >= col_l
    tri = jnp.where(causal, 1.0, 0.0).astype(BF16)
    tri_t = jnp.where(row_l <= col_l, 1.0, 0.0).astype(BF16)
    row_c = lax.broadcasted_iota(jnp.int32, (L, N_SMALL), 0)
    ones = jnp.ones((L, D_HEAD), F32)
    scale = D_HEAD ** -0.5
    heads = [slice(h * D_HEAD, (h + 1) * D_HEAD) for h in range(N_HEAD)]
    group = G

    def gates(bi, ci, rows):
        pre_c = sc_ref[bi, rows, :] + bias_c
        pre_r = sr_ref[bi, ci] + bias_r
        hi, mid, lo = _split3(_log_sigmoid(pre_c))
        b_c = (jnp.dot(tri, hi, preferred_element_type=F32) + jnp.dot(tri, mid, preferred_element_type=F32)
               + jnp.dot(tri, lo, preferred_element_type=F32))
        hi, mid, lo = _split3(_log_sigmoid(pre_r))
        b_r = (jnp.dot(hi, tri_t, preferred_element_type=F32) + jnp.dot(mid, tri_t, preferred_element_type=F32)
               + jnp.dot(lo, tri_t, preferred_element_type=F32))
        b = pltpu.roll(b_c, N_SMALL - N_HEAD, 1)
        a = pre_c - b
        cm = a
        s = 1
        while s < L:
            cm = jnp.maximum(cm, jnp.where(row_c >= s, pltpu.roll(cm, s, 0), -jnp.inf))
            s *= 2
        m_old = m_ref[bi]
        big_m = jnp.maximum(m_old, cm)
        m_last = big_m[L - 1:L, :]
        m_ref[bi] = b[L - 1:L, :] + m_last
        return dict(a_r=pre_r[0:N_HEAD, :] - b_r[N_HEAD:2 * N_HEAD, :], big_m=big_m, m_old=m_old,
                    w_inter=jnp.exp(m_old - big_m), n_exp=jnp.exp(-(b + big_m)),
                    w_state=jnp.exp(a - m_last), carry=jnp.exp(m_old - m_last))

    def body(ci, carry):
        rows = pl.ds(pl.multiple_of(ci * L, L), L)
        for g0 in range(0, G, group):
            seqs = range(g0, g0 + group)
            sh = {bi: gates(bi, ci, rows) for bi in seqs}
            probs = [(bi, h) for bi in seqs for h in range(N_HEAD)]
            qk, wi_b, intra, qc, qn = {}, {}, {}, {}, {}
            for p in probs:
                bi, h = p
                g = sh[bi]
                m_b = jnp.broadcast_to(g["big_m"][:, h:h + 1], (L, D_HEAD))
                wi_b[p] = jnp.exp(g["m_old"][:, h:h + 1] - m_b)
                w = jnp.where(causal, jnp.exp(g["a_r"][h:h + 1, :] - m_b[:, :L]), 0.0)
                q = q_ref[bi, rows, heads[h]] * scale
                qk[p] = _dot_nt(q, k_ref[bi, rows, heads[h]]) * w
            for p in probs:
                bi, h = p
                q = q_ref[bi, rows, heads[h]] * scale
                intra[p] = _dot(qk[p], jnp.concatenate([v_ref[bi, rows, heads[h]], ones], axis=1))
                qc[p] = _dot(q, c_ref[bi, h])
                qn[p] = _dot_nt(q, jnp.broadcast_to(n_ref[bi, h], (D_HEAD, D_HEAD)))
            for p in probs:
                bi, h = p
                g = sh[bi]
                num = intra[p][:, :D_HEAD] + wi_b[p] * qc[p]
                den = intra[p][:, D_HEAD + h:D_HEAD + h + 1] + g["w_inter"][:, h:h + 1] * qn[p][:, h:h + 1]
                r = 1.0 / jnp.maximum(jnp.abs(den), g["n_exp"][:, h:h + 1])
                t = r * lax.rsqrt(r * r * jnp.mean(num * num, axis=-1, keepdims=True) + EPS)
                gate = _sigmoid(og_ref[bi, rows, heads[h]])
                o_ref[bi, rows, heads[h]] = (num * t * nw[:, heads[h]] * gate).astype(o_ref.dtype)
            for p in probs:
                bi, h = p
                g = sh[bi]
                k = k_ref[bi, rows, heads[h]]
                kw = g["w_state"][:, h:h + 1] * k
                cd = g["carry"][:, h:h + 1]
                c_ref[bi, h] = cd * c_ref[bi, h] + _dot_tn(kw, v_ref[bi, rows, heads[h]])
                n_ref[bi, h] = cd * n_ref[bi, h] + jnp.sum(kw, axis=0, keepdims=True)
        return carry

    lax.fori_loop(0, nchunk, body, 0)


def _mlstm(proj3, small_row, bias_c, bias_r, nw, st_in, l_in, st_prev, l, depth, L, G, tt):
    b, t, _ = proj3.shape
    col = lambda c: pl.BlockSpec((G, tt, D_BR), lambda i, j: (i, j, c))
    full2 = lambda shape: pl.BlockSpec(shape, lambda i, j: (0, 0))
    shapes = [(G, N_HEAD, D_HEAD, D_HEAD), (G, N_HEAD, 1, D_HEAD), (G, 1, N_SMALL)]
    aliases = {}
    if st_prev is None:
        st_prev = [jnp.zeros((1,), F32)] * 3
    else:
        aliases = {12: 1, 13: 2, 14: 3}
    any_spec = pl.BlockSpec(memory_space=pl.ANY)
    return pl.pallas_call(
        functools.partial(_mlstm_kernel, L=L),
        grid=(b // G, t // tt),
        in_specs=[col(4), col(5), col(6), col(7),
                  pl.BlockSpec((G, tt, N_SMALL), lambda i, j: (i, j, COL_SMALL // N_SMALL)),
                  pl.BlockSpec((G, tt // L, 2 * N_HEAD, L), lambda i, j: (i, j, 0, 0)),
                  full2((1, N_SMALL)), full2((2 * N_HEAD, 1)), full2((1, D_BR))]
                 + [_layer_block(s, l_in, 2) for s in shapes] + [any_spec] * 3,
        out_specs=[pl.BlockSpec((G, tt, D_BR), lambda i, j: (i, j, 0))] + [_layer_block(s, l, 2) for s in shapes],
        out_shape=[jax.ShapeDtypeStruct((b, t, D_BR), BF16)]
                  + [jax.ShapeDtypeStruct((depth, b) + s[1:], F32) for s in shapes],
        input_output_aliases=aliases,
        compiler_params=_cparams(("parallel", "arbitrary")),
        name="mlstm",
    )(proj3, proj3, proj3, proj3, proj3, small_row, bias_c, bias_r, nw, *st_in, *st_prev)


PAD_C = 8
PAD_D = 32
CONV_ROWS = 32


def _conv_kernel(bc_ref, cc_ref, xc_ref, ad_ref, gd_ref, bufc_ref, bufd_ref, wc_ref, wd_ref, bd_ref,
                 lg_ref, lbias_ref, pc_ref, pd_ref, oc_ref, od_ref, nbc_ref, nbd_ref, extc, extd, shc, shd):
    del pc_ref, pd_ref
    bblk, tt, _ = bc_ref.shape
    hc, hd = CONV_C - 1, CONV_D - 1
    rt = min(tt, CONV_ROWS)
    offs_c = [PAD_C - hc + j for j in range(CONV_C)]
    offs_d = [PAD_D - hd + j for j in range(CONV_D)]
    unaligned_c = [o for o in offs_c if o % SUBLANES]
    span_d = tt + PAD_D - SUBLANES

    @pl.when(pl.program_id(1) == 0)
    def _():
        extc[:, PAD_C - hc:PAD_C, :] = bufc_ref[...]
        extd[:, PAD_D - hd:PAD_D, :] = bufd_ref[...]

    extc[:, PAD_C:PAD_C + tt, :] = cc_ref[...] * xc_ref[...]
    extd[:, PAD_D:PAD_D + tt, :] = ad_ref[...] * _sigmoid(gd_ref[...])

    def tile_rows(w8):
        return w8 if rt == SUBLANES else jnp.concatenate([w8] * (rt // SUBLANES), axis=0)

    def seq_body(bi, carry):
        for res in range(1, SUBLANES):
            shd[res, 0:span_d, :] = extd[bi, res:res + span_d, :]
        for i, o in enumerate(unaligned_c):
            shc[i, 0:tt, :] = extc[bi, o:o + tt, :]

        def tile_body(ti, c2):
            r0 = pl.multiple_of(ti * rt, rt)
            acc = jnp.zeros((rt, D_BR), F32)
            for j, o in enumerate(offs_c):
                if o % SUBLANES:
                    x = shc[unaligned_c.index(o), pl.ds(r0, rt), :]
                else:
                    x = extc[bi, pl.ds(pl.multiple_of(r0 + o, SUBLANES), rt), :]
                acc = acc + tile_rows(wc_ref[j]) * x
            oc_ref[bi, pl.ds(r0, rt), :] = (bc_ref[bi, pl.ds(r0, rt), :] * acc).astype(oc_ref.dtype)
            acc = jnp.zeros((rt, D_BR), F32)
            for j, o in enumerate(offs_d):
                res = o % SUBLANES
                rows = pl.ds(pl.multiple_of(r0 + (o - res), SUBLANES), rt)
                x = shd[res, rows, :] if res else extd[bi, rows, :]
                acc = acc + tile_rows(wd_ref[j]) * x
            y = acc + bd_ref[...]
            yc = y - jnp.mean(y, axis=-1, keepdims=True)
            z = yc * lax.rsqrt(jnp.mean(yc * yc, axis=-1, keepdims=True) + EPS) * lg_ref[...] + lbias_ref[...]
            od_ref[bi, pl.ds(r0, rt), :] = (z * _sigmoid(z)).astype(od_ref.dtype)
            return c2

        lax.fori_loop(0, tt // rt, tile_body, 0, unroll=4 if (tt // rt) % 4 == 0 else 1)
        return carry

    lax.fori_loop(0, bblk, seq_body, 0)

    new_c = extc[:, PAD_C + tt - hc:PAD_C + tt, :]
    new_d = extd[:, PAD_D + tt - hd:PAD_D + tt, :]
    nbc_ref[...] = new_c
    nbd_ref[...] = new_d
    extc[:, PAD_C - hc:PAD_C, :] = new_c
    extd[:, PAD_D - hd:PAD_D, :] = new_d


def _conv(proj3, buf_in, l_in, buf_prev, l, depth, wc, wd, bd, lg, lbias, bblk, tt):
    b, t, _ = proj3.shape
    c0 = COL_CONV // D_BR
    col = lambda c: pl.BlockSpec((bblk, tt, D_BR), lambda i, j: (i, j, c0 + c))
    full2 = lambda shape: pl.BlockSpec(shape, lambda i, j: (0, 0))
    full3 = lambda shape: pl.BlockSpec(shape, lambda i, j: (0, 0, 0))
    rows8 = lambda w: jnp.broadcast_to(w[:, None, :], (w.shape[0], SUBLANES, D_BR))
    shapes = [(bblk, CONV_C - 1, D_BR), (bblk, CONV_D - 1, D_BR)]
    outc =pl.BlockSpec((bblk, tt, D_BR), lambda i, j: (i, j, 0))
    aliases = {}
    if buf_prev is None:
        buf_prev = [jnp.zeros((1,), F32)] * 2
    else:
        aliases = {12: 2, 13: 3}
    any_spec = pl.BlockSpec(memory_space=pl.ANY)
    return pl.pallas_call(
        _conv_kernel,
        grid=(b // bblk, t // tt),
        in_specs=[col(0), col(1), col(2), col(3), col(4)] + [_layer_block(s, l_in, 2) for s in shapes]
                 + [full3((CONV_C, SUBLANES, D_BR)), full3((CONV_D, SUBLANES, D_BR)), full2((1, D_BR)),
                    full2((1, D_BR)), full2((1, D_BR)), any_spec, any_spec],
        out_specs=[outc, outc] + [_layer_block(s, l, 2) for s in shapes],
        out_shape=[jax.ShapeDtypeStruct((b, t, D_BR), BF16), jax.ShapeDtypeStruct((b, t, D_BR), BF16)]
                  + [jax.ShapeDtypeStruct((depth, b) + s[1:], F32) for s in shapes],
        scratch_shapes=[pltpu.VMEM((bblk, PAD_C + tt, D_BR), F32), pltpu.VMEM((bblk, PAD_D + tt, D_BR), F32),
                        pltpu.VMEM((CONV_C - 1, tt, D_BR), F32), pltpu.VMEM((SUBLANES, PAD_D + tt, D_BR), F32)],
        input_output_aliases=aliases,
        compiler_params=_cparams(("parallel", "arbitrary")),
        name="conv",
    )(proj3, proj3, proj3, proj3, proj3, *buf_in, rows8(wc), rows8(wd), bd, lg, lbias, *buf_prev)


FF_TILE = 1024
MIX_VMEM_LIMIT = 60 * 1024 * 1024


def _mix_kernel(x_ref, oa_ref, ob_ref, oc_ref, od_ref, gt_ref, wb_ref, wo_ref, nm_ref, wu_ref, wdn_ref,
                nn_ref, *out_refs, last):
    merged = None
    for n, br in enumerate((oa_ref, ob_ref, oc_ref, od_ref)):
        gate = _sigmoid(gt_ref[:, n * D_MODEL:(n + 1) * D_MODEL])
        term = gate * jnp.dot(br[...], wb_ref[n], preferred_element_type=F32)
        merged = term if merged is None else merged + term
    x = x_ref[...] + jnp.dot(merged.astype(BF16), wo_ref[...], preferred_element_type=F32)
    hm = _rms(x, nm_ref[...]).astype(BF16)
    for c in range(D_FF // FF_TILE):
        cs = slice(c * FF_TILE, (c + 1) * FF_TILE)
        up = jnp.maximum(jnp.dot(hm, wu_ref[:, cs], preferred_element_type=F32), 0.0)
        x = x + jnp.dot((up * up).astype(BF16), wdn_ref[cs, :], preferred_element_type=F32)
    if last:
        out_refs[0][...] = _rms(x, nn_ref[...])
    else:
        out_refs[0][...] = x
        out_refs[1][...] = _rms(x, nn_ref[...]).astype(BF16)


def _mix(x, oa, ob, oc, od, proj, wb, wo, nm, wu, wdn, nn, l, last):
    n = x.shape[0]
    tm = min(n, ROW_TILE)
    row = lambda w: pl.BlockSpec((tm, w), lambda i: (i, 0))
    once = pl.Buffered(1)
    full2 = lambda shape: pl.BlockSpec(shape, lambda i: (0, 0))
    layer3 = lambda shape: pl.BlockSpec((None,) + shape, lambda i: (l, 0, 0), pipeline_mode=once)
    out_specs = [row(D_MODEL)] if last else [row(D_MODEL), row(D_MODEL)]
    out_shape = [jax.ShapeDtypeStruct((n, D_MODEL), F32)]
    if not last:
        out_shape.append(jax.ShapeDtypeStruct((n, D_MODEL), BF16))
    return pl.pallas_call(
        functools.partial(_mix_kernel, last=last),
        grid=(n // tm,),
        in_specs=[row(D_MODEL), row(D_BR), row(D_BR), row(D_BR), row(D_BR),
                  pl.BlockSpec((tm, N_GATE), lambda i: (i, COL_GATE // N_GATE)),
                  pl.BlockSpec((None, N_BRANCH, D_BR, D_MODEL), lambda i: (l, 0, 0, 0), pipeline_mode=once),
                  layer3((D_MODEL, D_MODEL)), full2((1, D_MODEL)), layer3((D_MODEL, D_FF)),
                  layer3((D_FF, D_MODEL)), full2((1, D_MODEL))],
        out_specs=out_specs,
        out_shape=out_shape,
        compiler_params=_cparams(("parallel",), MIX_VMEM_LIMIT),
        name="mix",
    )(x, oa, ob, oc, od, proj, wb, wo, nm, wu, wdn, nn)


def _layer(x, h, b, t, st_in, l_in, st_prev, l, depth, lw, wts, last):
    (lb, n_hgrn, bias_c, bias_r, n_mlstm, wc, wd, bd, lg, lbias, n_mlp, n_next) = lw
    w_in, wb, wo, wu, wdn = wts
    L = math.gcd(t, CHUNK)
    G = min(b, SEQ_GROUP)
    proj = _proj(h, w_in, l)
    proj3 = proj.reshape(b, t, D_PROJ)
    small = proj3[:, :, COL_SMALL:COL_SMALL + 2 * N_HEAD]
    small_row = jnp.swapaxes(small.reshape(b, t // L, L, 2 * N_HEAD), 2, 3)

    prev = (None,) * 6 if st_prev is None else st_prev
    g_h, tt_h = (4, 256) if t >= 512 else (G, t)
    oa, s_hgrn = _hgrn(proj3, lb, n_hgrn, st_in[0], l_in, prev[0], l, depth, L, g_h, tt_h)
    tt_m = min(t, 128)
    ob, s_c, s_n, s_m = _mlstm(proj3, small_row, bias_c, bias_r, n_mlstm, st_in[1:4], l_in,
                               None if st_prev is None else prev[1:4], l, depth, L, G, tt_m)
    g_c, tt_c = (1, 512) if t >= 512 else (min(b, 4 * SEQ_GROUP), t)
    oc, od, buf_c, buf_d = _conv(proj3, st_in[4:6], l_in, None if st_prev is None else prev[4:6], l, depth,
                                 wc, wd, bd, lg, lbias, g_c, tt_c)
    flat = lambda a: a.reshape(b * t, D_BR)
    outs = _mix(x, flat(oa), flat(ob), flat(oc), flat(od), proj, wb, wo, n_mlp, wu, wdn, n_next, l, last)
    return outs, (s_hgrn, s_c, s_n, s_m, buf_c, buf_d)


def kernel(x_prompt, x_sample, state_hgrn, state_mlstm_c, state_mlstm_n, state_mlstm_m, state_conv_short, state_conv_conformer, w_in, w_branch, w_out, lb_logits, norm_hgrn, mlstm_i_bias, mlstm_f_bias, norm_mlstm, conv_short_w, conv_conformer_w, conv_conformer_b, ln_conformer_g, ln_conformer_b, norm_mix, norm_mlp, w_up, w_down, norm_final):
    depth = w_in.shape[0]
    bp, tp, _ = x_prompt.shape
    bs, ts, _ = x_sample.shape
    lower_bounds = _lower_bounds(lb_logits)

    wts = (_permute_w_in(jnp.swapaxes(w_in, 1, 2)),) + tuple(a.astype(BF16) for a in (w_branch, w_out, w_up, w_down))
    gate_bias = jnp.concatenate([mlstm_i_bias, mlstm_f_bias], axis=1).astype(F32)
    bias_c = jnp.pad(gate_bias, ((0, 0), (0, N_SMALL - 2 * N_HEAD)))[:, None, :]
    bias_r = gate_bias[:, :, None]
    row = lambda a, l: a[l][None, :].astype(F32)

    zero_st = (jnp.zeros((1, bp, N_HEAD, D_HEAD, D_HEAD), F32), jnp.zeros((1, bp, N_HEAD, D_HEAD, D_HEAD), F32),
               jnp.zeros((1, bp, N_HEAD, 1, D_HEAD), F32), jnp.zeros((1, bp, 1, N_SMALL), F32),
               jnp.zeros((1, bp, CONV_C - 1, D_BR), F32), jnp.zeros((1, bp, CONV_D - 1, D_BR), F32))
    m_lanes = jnp.pad(state_mlstm_m.astype(F32), ((0, 0), (0, 0), (0, N_SMALL - N_HEAD)))[:, :, None, :]
    past_st = (state_hgrn, state_mlstm_c, state_mlstm_n[:, :, :, None, :], m_lanes,
               state_conv_short, state_conv_conformer)

    xp = x_prompt.reshape(bp * tp, D_MODEL)
    xs = x_sample.reshape(bs * ts, D_MODEL)
    hp = _norm(xp, row(norm_mix, 0))
    hs = _norm(xs, row(norm_mix, 0))
    st_p = st_s = None
    for l in range(depth):
        last = l == depth - 1
        n_next = norm_final[None, :].astype(F32) if last else row(norm_mix, l + 1)
        lw = (lower_bounds[l][None, :], row(norm_hgrn, l), bias_c[l], bias_r[l], row(norm_mlstm, l),
              conv_short_w[l], conv_conformer_w[l], row(conv_conformer_b, l), row(ln_conformer_g, l),
              row(ln_conformer_b, l), row(norm_mlp, l), n_next)
        outs_p, st_p = _layer(xp, hp, bp, tp, zero_st, 0, st_p, l, depth, lw, wts, last)
        outs_s, st_s = _layer(xs, hs, bs, ts, past_st, l, st_s, l, depth, lw, wts, last)
        if last:
            xp, xs = outs_p[0], outs_s[0]
        else:
            (xp, hp), (xs, hs) = outs_p, outs_s

    def finish(x, b, t, st):
        s_hgrn, s_c, s_n, s_m, buf_c, buf_d = st
        return (x.reshape(b, t, D_MODEL), s_hgrn, s_c, s_n[:, :, :, 0, :], s_m[:, :, 0, :N_HEAD], buf_c, buf_d)

    fp = finish(xp, bp, tp, st_p)
    fs = finish(xs, bs, ts, st_s)
    outs = []
    for a, c in zip(fp, fs):
        outs += [a, c]
    return tuple(outs)
```
